```python
import jax, jax.numpy as jnp
from jax import lax
import numpy as np

D_MODEL = 1024
BATCH = 8
SEQ = 4096
DEPTH = 1
DEC_BATCH = 32
DEC_SEQ = 16
PAST_LEN = 1024

CHUNK = 64

D_CONV = 512
CONV_W = 3

RWKV_HEADS = 8
HEAD_DIM = 64
D_RWKV = RWKV_HEADS * HEAD_DIM
LORA_DECAY = 64
LORA_AAA = 64
LORA_GATE = 128
GN_EPS = 64e-5

OFF_RKV = 3 * D_CONV
OFF_GATE = OFF_RKV + 3 * D_RWKV
D_IN = OFF_GATE + 2 * D_MODEL

PEER_HEADS = 8
PEER_KEYS = 128
N_EXPERTS = PEER_KEYS * PEER_KEYS
PEER_QDIM = 256
PEER_HALF = PEER_QDIM // 2
PEER_TOPK = 16
PEER_BLOCK = 128

RMS_EPS = 1e-6

kernel_name = "hybrid_conv_rwkv7_peer_stream_step"


def rms_norm(x, g):
    xf = x.astype(jnp.float32)
    y = xf * lax.rsqrt(jnp.mean(xf * xf, axis=-1, keepdims=True) + RMS_EPS)
    return (y * g.astype(jnp.float32)).astype(x.dtype)


def wkv7_scan(s0, r, w, k, v, a, b):
    def step(S, inp):
        rt, wt, kt, vt, at, bt = inp
        sa = jnp.einsum('bhvk,bhk->bhv', S, at)
        S = S * wt[:, :, None, :] + sa[..., None] * bt[:, :, None, :] + vt[..., None] * kt[:, :, None, :]
        return S, jnp.einsum('bhvk,bhk->bhv', S, rt)
    seq = tuple(jnp.swapaxes(t, 0, 1) for t in (r, w, k, v, a, b))
    s_fin, y = lax.scan(step, s0, seq)
    return jnp.swapaxes(y, 0, 1), s_fin


def hybrid_mixer(xn, conv_state, shift_state, wkv_state, w_in, conv_w, mu_rkv, mu_wag,
                 w0, w1, w2, a0, a1, a2, g1, g2, k_k, k_a, r_k, gn_w, gn_b, w_pa, w_pb, w_o):
    bsz, T, _ = xn.shape
    dt = xn.dtype
    z = xn @ w_in
    zb = z[..., 0:D_CONV]
    zc = z[..., D_CONV:2 * D_CONV]
    zh = z[..., 2 * D_CONV:3 * D_CONV]
    zrkv = z[..., OFF_RKV:OFF_GATE]
    zga = z[..., OFF_GATE:OFF_GATE + D_MODEL]
    zgb = z[..., OFF_GATE + D_MODEL:]

    u = zc * zh
    u_full = jnp.concatenate([conv_state.astype(dt), u], axis=1)
    conv = sum(conv_w[j] * u_full[:, j:j + T] for j in range(CONV_W))
    y_a = zb * conv
    new_conv = u_full[:, T:]

    prev_row = shift_state.astype(dt)
    xprev = jnp.concatenate([prev_row[:, None], xn[:, :-1]], axis=1)
    zrkv_prev = jnp.concatenate([(prev_row @ w_in[:, OFF_RKV:OFF_GATE])[:, None], zrkv[:, :-1]], axis=1)
    zs = zrkv + mu_rkv * (zrkv_prev - zrkv)
    dx = xprev - xn
    xw = xn + dx * mu_wag[0]
    xa = xn + dx * mu_wag[1]
    xg = xn + dx * mu_wag[2]
    w_raw = -jax.nn.softplus(-(w0 + jnp.tanh(xw @ w1) @ w2)) - 0.5
    decay = jnp.exp(-jnp.exp(w_raw.astype(jnp.float32)))
    a = jax.nn.sigmoid((a0 + (xa @ a1) @ a2).astype(jnp.float32))
    g = jax.nn.sigmoid(xg @ g1) @ g2

    hs = (bsz, T, RWKV_HEADS, HEAD_DIM)
    r = zs[..., 0:D_RWKV].astype(jnp.float32).reshape(hs)
    k = zs[..., D_RWKV:2 * D_RWKV].astype(jnp.float32)
    v = zs[..., 2 * D_RWKV:].astype(jnp.float32).reshape(hs)
    kk = (k * k_k.astype(jnp.float32)).reshape(hs)
    kk = kk / jnp.maximum(jnp.sqrt(jnp.sum(kk * kk, axis=-1, keepdims=True)), 1e-12)
    k = (k * (1.0 + (a - 1.0) * k_a.astype(jnp.float32))).reshape(hs)
    a = a.reshape(hs)
    w = decay.reshape(hs)
    y, s_fin = wkv7_scan(wkv_state.astype(jnp.float32), r, w, k, v, -kk, kk * a)
    mean = jnp.mean(y, axis=-1, keepdims=True)
    var = jnp.mean(jnp.square(y - mean), axis=-1, keepdims=True)
    y = ((y - mean) * lax.rsqrt(var + GN_EPS)).reshape(bsz, T, D_RWKV)
    y = y * gn_w.astype(jnp.float32) + gn_b.astype(jnp.float32)
    bonus = jnp.sum(r * k * r_k.astype(jnp.float32), axis=-1, keepdims=True) * v
    y = y + bonus.reshape(bsz, T, D_RWKV)
    y_b = y.astype(dt) * g

    merged = jax.nn.sigmoid(zga) * (y_a @ w_pa) + jax.nn.sigmoid(zgb) * (y_b @ w_pb)
    out = merged @ w_o
    return out, new_conv, xn[:, -1], s_fin.astype(wkv_state.dtype)


def peer_ffn(x, wq, keys, U, V):
    bsz, T, D = x.shape
    n = bsz * T
    nb = -(-n // PEER_BLOCK)
    pad = nb * PEER_BLOCK - n
    xb = jnp.pad(x.reshape(n, D), ((0, pad), (0, 0))).reshape(nb, PEER_BLOCK, D)

    def block(xt):
        q = (xt @ wq).reshape(PEER_BLOCK, PEER_HEADS, 2, PEER_HALF)
        s = jnp.einsum('thpd,hpnd->thpn', q, keys).astype(jnp.float32)
        s_top, i_top = lax.top_k(s, PEER_TOPK)
        cand = (s_top[:, :, 0, :, None] + s_top[:, :, 1, None, :]).reshape(PEER_BLOCK, PEER_HEADS, PEER_TOPK * PEER_TOPK)
        cidx = (i_top[:, :, 0, :, None] * PEER_KEYS + i_top[:, :, 1, None, :]).reshape(PEER_BLOCK, PEER_HEADS, PEER_TOPK * PEER_TOPK)
        sc, pos = lax.top_k(cand, PEER_TOPK)
        eidx = jnp.take_along_axis(cidx, pos, axis=-1)
        gate = jax.nn.softmax(sc, axis=-1)
        u = jnp.take(U, eidx, axis=0)
        act = jax.nn.gelu(jnp.einsum('thkd,td->thk', u, xt).astype(jnp.float32))
        coef = (gate * act).astype(xt.dtype)
        return jnp.einsum('thk,thkd->td', coef, jnp.take(V, eidx, axis=0))

    out = lax.map(block, xb).reshape(nb * PEER_BLOCK, D)[:n]
    return out.reshape(bsz, T, D)


def run_trunk(x, st_conv, st_shift, st_wkv, weights):
    (norm1_g, w_in, conv_w, mu_rkv, mu_wag, w0, w1, w2, a0, a1, a2, g1, g2, k_k, k_a, r_k,
     gn_w, gn_b, w_pa, w_pb, w_o, norm2_g, peer_wq, peer_keys, peer_u, peer_v, norm_f_g) = weights
    convs, shifts, wkvs = [], [], []
    for l in range(DEPTH):
        xn = rms_norm(x, norm1_g[l])
        h, c_new, s_new, w_new = hybrid_mixer(
            xn, st_conv[l], st_shift[l], st_wkv[l], w_in[l], conv_w[l], mu_rkv[l], mu_wag[l],
            w0[l], w1[l], w2[l], a0[l], a1[l], a2[l], g1[l], g2[l], k_k[l], k_a[l], r_k[l],
            gn_w[l], gn_b[l], w_pa[l], w_pb[l], w_o[l])
        x = x + h
        x = x + peer_ffn(rms_norm(x, norm2_g[l]), peer_wq[l], peer_keys[l], peer_u[l], peer_v[l])
        convs.append(c_new)
        shifts.append(s_new)
        wkvs.append(w_new)
    return rms_norm(x, norm_f_g), jnp.stack(convs), jnp.stack(shifts), jnp.stack(wkvs)


def setup_inputs(seed: int = 0) -> dict:
    key = jax.random.key(seed)
    ks = jax.random.split(key, 40)
    f32 = jnp.float32

    def nrm(k, shape, scale):
        return jax.random.normal(k, shape, f32) * scale

    L = DEPTH
    return {
        "x_prompt": nrm(ks[0], (BATCH, SEQ, D_MODEL), 1.0),
        "x_sample": nrm(ks[1], (DEC_BATCH, DEC_SEQ, D_MODEL), 1.0),
        "state_conv": nrm(ks[2], (L, DEC_BATCH, CONV_W - 1, D_CONV), 1.0),
        "state_shift": nrm(ks[3], (L, DEC_BATCH, D_MODEL), 1.0),
        "state_wkv": nrm(ks[4], (L, DEC_BATCH, RWKV_HEADS, HEAD_DIM, HEAD_DIM), 0.1),
        "norm1_g": 1.0 + nrm(ks[5], (L, D_MODEL), 0.02),
        "w_in": nrm(ks[6], (L, D_MODEL, D_IN), D_MODEL ** -0.5),
        "conv_w": nrm(ks[7], (L, CONV_W, D_CONV), 0.5),
        "mu_rkv": jax.random.uniform(ks[8], (L, 3 * D_RWKV), f32),
        "mu_wag": jax.random.uniform(ks[9], (L, 3, D_MODEL), f32),
        "w0": -1.0 + nrm(ks[10], (L, D_RWKV), 1.0),
        "w1": nrm(ks[11], (L, D_MODEL, LORA_DECAY), D_MODEL ** -0.5),
        "w2": nrm(ks[12], (L, LORA_DECAY, D_RWKV), 0.5 * LORA_DECAY ** -0.5),
        "a0": nrm(ks[13], (L, D_RWKV), 0.5),
        "a1": nrm(ks[14], (L, D_MODEL, LORA_AAA), D_MODEL ** -0.5),
        "a2": nrm(ks[15], (L, LORA_AAA, D_RWKV), 0.5 * LORA_AAA ** -0.5),
        "g1": nrm(ks[16], (L, D_MODEL, LORA_GATE), D_MODEL ** -0.5),
        "g2": nrm(ks[17], (L, LORA_GATE, D_RWKV), LORA_GATE ** -0.5),
        "k_k": 0.85 + nrm(ks[18], (L, D_RWKV), 0.05),
        "k_a": 1.0 + nrm(ks[19], (L, D_RWKV), 0.05),
        "r_k": nrm(ks[20], (L, RWKV_HEADS, HEAD_DIM), 0.1),
        "gn_w": 1.0 + nrm(ks[21], (L, D_RWKV), 0.02),
        "gn_b": nrm(ks[22], (L, D_RWKV), 0.02),
        "w_pa": nrm(ks[23], (L, D_CONV, D_MODEL), D_CONV ** -0.5),
        "w_pb": nrm(ks[24], (L, D_RWKV, D_MODEL), D_RWKV ** -0.5),
        "w_o": nrm(ks[25], (L, D_MODEL, D_MODEL), D_MODEL ** -0.5),
        "norm2_g": 1.0 + nrm(ks[26], (L, D_MODEL), 0.02),
        "peer_wq": nrm(ks[27], (L, D_MODEL, PEER_HEADS * PEER_QDIM), D_MODEL ** -0.5),
        "peer_keys": nrm(ks[28], (L, PEER_HEADS, 2, PEER_KEYS, PEER_HALF), PEER_HALF ** -0.5),
        "peer_u": nrm(ks[29], (L, N_EXPERTS, D_MODEL), D_MODEL ** -0.5),
        "peer_v": nrm(ks[30], (L, N_EXPERTS, D_MODEL), 0.5 * PEER_HEADS ** -0.5),
        "norm_f_g": 1.0 + nrm(ks[31], (D_MODEL,), 0.02),
    }


def reference(x_prompt, x_sample, state_conv, state_shift, state_wkv,
              norm1_g, w_in, conv_w, mu_rkv, mu_wag, w0, w1, w2, a0, a1, a2, g1, g2,
              k_k, k_a, r_k, gn_w, gn_b, w_pa, w_pb, w_o, norm2_g,
              peer_wq, peer_keys, peer_u, peer_v, norm_f_g):
    weights = (norm1_g, w_in, conv_w, mu_rkv, mu_wag, w0, w1, w2, a0, a1, a2, g1, g2, k_k, k_a, r_k,
               gn_w, gn_b, w_pa, w_pb, w_o, norm2_g, peer_wq, peer_keys, peer_u, peer_v, norm_f_g)
    bp = x_prompt.shape[0]
    dt = x_prompt.dtype
    zero_conv = jnp.zeros((DEPTH, bp, CONV_W - 1, D_CONV), dt)
    zero_shift = jnp.zeros((DEPTH, bp, D_MODEL), dt)
    zero_wkv = jnp.zeros((DEPTH, bp, RWKV_HEADS, HEAD_DIM, HEAD_DIM), state_wkv.dtype)
    y_prompt, new_conv_prompt, new_shift_prompt, new_wkv_prompt = run_trunk(
        x_prompt, zero_conv, zero_shift, zero_wkv, weights)
    y_sample, new_conv_sample, new_shift_sample, new_wkv_sample = run_trunk(
        x_sample, state_conv, state_shift, state_wkv, weights)
    return (y_prompt, y_sample, new_conv_prompt, new_shift_prompt, new_wkv_prompt,
            new_conv_sample, new_shift_sample, new_wkv_sample)
```

```python
import functools

import jax
import jax.numpy as jnp
from jax import lax
from jax.experimental import pallas as pl
from jax.experimental.pallas import tpu as pltpu

F32 = jnp.float32
BF16 = jnp.bfloat16

D_MODEL = 1024
D_CONV = 512
CONV_W = 3
HEADS = 8
HEAD_DIM = 64
D_RWKV = HEADS * HEAD_DIM
PAIR = 2 * HEAD_DIM
N_PAIRS = HEADS // 2
GN_EPS = 64e-5
RMS_EPS = 1e-6
OFF_RKV = 3 * D_CONV
OFF_GATE = OFF_RKV + 3 * D_RWKV
D_IN = OFF_GATE + 2 * D_MODEL

PEER_HEADS = 8
PEER_KEYS = 128
PEER_HALF = 128
PEER_TOPK = 16
PEER_SEL = PEER_HEADS * PEER_TOPK
TOK_TILE = 128

VMEM_LIMIT_BYTES = 56 * 1024 * 1024


def _dot(a, b):
    return jnp.dot(a.astype(BF16), b.astype(BF16), preferred_element_type=F32)


def _dot_nt(a, b):
    return lax.dot_general(a.astype(BF16), b.astype(BF16), (((1,), (1,)), ((), ())),
                           preferred_element_type=F32)


def _dot_tn(a, b):
    return lax.dot_general(a.astype(BF16), b.astype(BF16), (((0,), (0,)), ((), ())),
                           preferred_element_type=F32)


def _split(a):
    hi = a.astype(BF16)
    lo = (a - hi.astype(F32)).astype(BF16)
    return hi, lo


def _dot_hl(a, w_bf16):
    hi, lo = _split(a)
    return (jnp.dot(hi, w_bf16, preferred_element_type=F32)
            + jnp.dot(lo, w_bf16, preferred_element_type=F32))


def _sigmoid(x):
    return 1.0 / (1.0 + jnp.exp(-x))


def _rms_norm(x, g):
    return x * lax.rsqrt(jnp.mean(x * x, axis=-1, keepdims=True) + RMS_EPS) * g


def _shift_rows(a, carry, n):
    rolled = pltpu.roll(a, n, 0)
    row = lax.broadcasted_iota(jnp.int32, a.shape, 0)
    for i in range(n):
        rolled = jnp.where(row == i, carry[i:i + 1], rolled)
    return rolled


def _pre_kernel(x_ref, conv0_ref, shift0_ref, n1g_ref, w_in_ref, convw_ref, mu_rkv_ref, mu_wag_ref,
                w0_ref, w1_ref, w2_ref, a0_ref, a1_ref, a2_ref, g1_ref, g2_ref, kk_ref, ka_ref, rk_ref,
                w_pa_ref, hsum_ref,
                r_out, lw_out, k_out, v_out, a_out, b_out, bonus_out, g_out, ma_out, sgb_out,
                nshift_out, nconv_out,
                xn_c, zrkv_c, u_c):
    t = pl.program_id(1)
    tt = x_ref.shape[1]
    xn = _rms_norm(x_ref[0], n1g_ref[...])
    xnb = xn.astype(BF16)

    @pl.when(t == 0)
    def _():
        prev = jnp.broadcast_to(shift0_ref[0], (8, D_MODEL))
        xn_c[...] = prev
        zrkv_c[...] = jnp.dot(prev.astype(BF16), w_in_ref[:, OFF_RKV:OFF_GATE],
                              preferred_element_type=F32)
        u_c[0:2, :] = conv0_ref[0]

    zbch = jnp.dot(xnb, w_in_ref[:, 0:OFF_RKV], preferred_element_type=F32)
    zb = zbch[:, 0:D_CONV]
    u = zbch[:, D_CONV:2 * D_CONV] * zbch[:, 2 * D_CONV:3 * D_CONV]
    u_prev = u_c[0:2, :]
    u1 = _shift_rows(u, u_prev[1:2], 1)
    u2 = _shift_rows(u, u_prev, 2)
    cw = convw_ref[...]
    y_a = zb * (cw[0:1] * u2 + cw[1:2] * u1 + cw[2:3] * u)
    u_last = u[tt - 2:tt, :]
    nconv_out[0] = u_last
    u_c[0:2, :] = u_last

    zg = jnp.dot(xnb, w_in_ref[:, OFF_GATE:D_IN], preferred_element_type=F32)
    ma_out[0] = _sigmoid(zg[:, 0:D_MODEL]) * _dot(y_a, w_pa_ref[...])
    sgb_out[0] = _sigmoid(zg[:, D_MODEL:2 * D_MODEL])

    zrkv = jnp.dot(xnb, w_in_ref[:, OFF_RKV:OFF_GATE], preferred_element_type=F32)
    zprev = _shift_rows(zrkv, zrkv_c[0:1, :], 1)
    zs = zrkv + mu_rkv_ref[...] * (zprev - zrkv)
    xprev = _shift_rows(xn, xn_c[0:1, :], 1)
    dx = xprev - xn
    mu = mu_wag_ref[...]
    xw = xn + dx * mu[0:1]
    xa = xn + dx * mu[1:2]
    xg = xn + dx * mu[2:3]
    xn_last = xn[tt - 1:tt, :]
    nshift_out[0] = xn_last
    xn_c[0:1, :] = xn_last
    zrkv_c[0:1, :] = zrkv[tt - 1:tt, :]

    wl = w0_ref[...] + _dot(jnp.tanh(_dot(xw, w1_ref[...])), w2_ref[...])
    softplus = jnp.maximum(-wl, 0.0) + jnp.log(1.0 + jnp.exp(-jnp.abs(wl)))
    lw_out[0] = -jnp.exp(-softplus - 0.5)
    a_sig = _sigmoid(a0_ref[...] + _dot(_dot(xa, a1_ref[...]), a2_ref[...]))
    g_out[0] = _dot(_sigmoid(_dot(xg, g1_ref[...])), g2_ref[...])

    r = zs[:, 0:D_RWKV]
    k = zs[:, D_RWKV:2 * D_RWKV]
    v = zs[:, 2 * D_RWKV:3 * D_RWKV]
    hsum = hsum_ref[...]
    kk = k * kk_ref[...]
    kk = kk / jnp.maximum(jnp.sqrt(_dot_hl(kk * kk, hsum)), 1e-12)
    k = k * (1.0 + (a_sig - 1.0) * ka_ref[...])
    r_out[0] = r
    k_out[0] = k
    v_out[0] = v
    a_out[0] = -kk
    b_out[0] = kk * a_sig
    bonus_out[0] = _dot_hl(r * k * rk_ref[...], hsum) * v


def _pre_call(x, conv0, shift0, w, tt):
    bsz, seq, _ = x.shape
    grid = (bsz, seq // tt)
    row = lambda b, t: (b, t, 0)
    per_b = lambda b, t: (b, 0, 0)
    const2 = lambda b, t: (0, 0)

    def tok(c):
        return pl.BlockSpec((1, tt, c), row)

    def full(a):
        return pl.BlockSpec(a.shape, const2)

    weights = (w["norm1_g"], w["w_in"], w["conv_w"], w["mu_rkv"], w["mu_wag"], w["w0"], w["w1"], w["w2"],
               w["a0"], w["a1"], w["a2"], w["g1"], w["g2"], w["k_k"], w["k_a"], w["r_k"], w["w_pa"],
               w["hsum"])
    in_specs = [tok(D_MODEL), pl.BlockSpec((1, CONV_W - 1, D_CONV), per_b),
                pl.BlockSpec((1, 1, D_MODEL), per_b)] + [full(a) for a in weights]
    tok_shape = lambda c: jax.ShapeDtypeStruct((bsz, seq, c), F32)
    out_shape = [tok_shape(D_RWKV)] * 8 + [tok_shape(D_MODEL)] * 2 + [
        jax.ShapeDtypeStruct((bsz, 1, D_MODEL), F32),
        jax.ShapeDtypeStruct((bsz, CONV_W - 1, D_CONV), F32)]
    out_specs = [tok(D_RWKV)] * 8 + [tok(D_MODEL)] * 2 + [
        pl.BlockSpec((1, 1, D_MODEL), per_b), pl.BlockSpec((1, CONV_W - 1, D_CONV), per_b)]
    return pl.pallas_call(
        _pre_kernel, grid=grid, in_specs=in_specs, out_specs=out_specs, out_shape=out_shape,
        scratch_shapes=[pltpu.VMEM((8, D_MODEL), F32), pltpu.VMEM((8, 3 * D_RWKV), F32),
                        pltpu.VMEM((8, D_CONV), F32)],
        compiler_params=pltpu.CompilerParams(dimension_semantics=("arbitrary", "arbitrary"),
                                             vmem_limit_bytes=VMEM_LIMIT_BYTES),
        name="pre",
    )(x, conv0, shift0, *weights)


def _wkv_kernel(r_ref, lw_ref, k_ref, v_ref, a_ref, b_ref, s0_ref, tri_ref,
                y_out, s_out, s_c):
    c = pl.program_id(2)
    L = r_ref.shape[1]
    lane = lax.broadcasted_iota(jnp.int32, (L, PAIR), 1)
    first = lane < HEAD_DIM
    s_row = lax.broadcasted_iota(jnp.int32, (PAIR, PAIR), 0)
    s_col = lax.broadcasted_iota(jnp.int32, (PAIR, PAIR), 1)
    s_mask = (s_row < HEAD_DIM) == (s_col < HEAD_DIM)

    @pl.when(c == 0)
    def _():
        s0 = s0_ref[0]
        z = jnp.zeros((HEAD_DIM, HEAD_DIM), F32)
        s_c[...] = jnp.concatenate([jnp.concatenate([s0[0], z], axis=1),
                                    jnp.concatenate([z, s0[1]], axis=1)], axis=0)

    S = s_c[...]
    r = r_ref[0]
    lw = lw_ref[0]
    k = k_ref[0]
    v = v_ref[0]
    a = a_ref[0]
    b = b_ref[0]

    l1 = lw.astype(BF16)
    r1 = lw - l1.astype(F32)
    l2 = r1.astype(BF16)
    l3 = (r1 - l2.astype(F32)).astype(BF16)
    tri = tri_ref[...]
    cum = (jnp.dot(tri, l1, preferred_element_type=F32) + jnp.dot(tri, l2, preferred_element_type=F32)
           + jnp.dot(tri, l3, preferred_element_type=F32))
    cum_l = cum[L - 1:L, :]
    w_inc = jnp.exp(cum)
    w_inv = jnp.exp(-cum)
    at = a * jnp.exp(cum - lw)
    bt = b * w_inv
    kt = k * w_inv
    rt = r * w_inc
    dec = jnp.exp(cum_l - cum)

    row = lax.broadcasted_iota(jnp.int32, (L, L), 0)
    col = lax.broadcasted_iota(jnp.int32, (L, L), 1)
    strict = row > col
    incl = row >= col

    def per_head(x):
        return jnp.where(first, x, 0.0), jnp.where(first, 0.0, x)

    def merge(x1, x2):
        return jnp.where(first, x1, x2)

    at1, at2 = per_head(at)
    rt1, rt2 = per_head(rt)
    zero = jnp.zeros((L, L), F32)
    mab = [jnp.where(strict, _dot_nt(x, bt), zero) for x in (at1, at2)]
    mak = [jnp.where(strict, _dot_nt(x, kt), zero) for x in (at1, at2)]
    nrb = [jnp.where(incl, _dot_nt(x, bt), zero) for x in (rt1, rt2)]
    nrk = [jnp.where(incl, _dot_nt(x, kt), zero) for x in (rt1, rt2)]

    U = _dot_nt(at, S) + merge(_dot(mak[0], v), _dot(mak[1], v))
    n = 1
    while n < L:
        U = U + merge(_dot(mab[0], U), _dot(mab[1], U))
        n *= 2
        if n < L:
            mab = [_dot(m, m) for m in mab]
    y = (_dot_nt(rt, S) + merge(_dot(nrb[0], U), _dot(nrb[1], U))
         + merge(_dot(nrk[0], v), _dot(nrk[1], v)))
    y_out[0] = y

    s_new = S * jnp.exp(cum_l) + jnp.where(s_mask, _dot_tn(U, b * dec) + _dot_tn(v, k * dec), 0.0)
    s_c[...] = s_new
    s_out[0, 0] = s_new[0:HEAD_DIM, 0:HEAD_DIM]
    s_out[0, 1] = s_new[HEAD_DIM:PAIR, HEAD_DIM:PAIR]


def _wkv_call(r, lw, k, v, a, b, s0, chunk):
    bsz, seq, _ = r.shape
    grid = (bsz, N_PAIRS, seq // chunk)
    tok = pl.BlockSpec((1, chunk, PAIR), lambda bi, p, c: (bi, c, p))
    st = pl.BlockSpec((1, 2, HEAD_DIM, HEAD_DIM), lambda bi, p, c: (bi, p, 0, 0))
    tri = (jnp.arange(chunk)[:, None] >= jnp.arange(chunk)[None, :]).astype(BF16)
    return pl.pallas_call(
        _wkv_kernel, grid=grid,
        in_specs=[tok] * 6 + [st, pl.BlockSpec((chunk, chunk), lambda bi, p, c: (0, 0))],
        out_specs=[tok, st],
        out_shape=[jax.ShapeDtypeStruct((bsz, seq, D_RWKV), F32),
                   jax.ShapeDtypeStruct((bsz, HEADS, HEAD_DIM, HEAD_DIM), F32)],
        scratch_shapes=[pltpu.VMEM((PAIR, PAIR), F32)],
        compiler_params=pltpu.CompilerParams(
            dimension_semantics=("arbitrary", "arbitrary", "arbitrary")),
        name="wkv",
    )(r, lw, k, v, a, b, s0, tri)


def _post_kernel(y_ref, bonus_ref, g_ref, ma_ref, sgb_ref, x_ref, gnw_ref, gnb_ref, hsum_ref,
                 w_pb_ref, w_o_ref, n2g_ref, wq_hi_ref, wq_lo_ref,
                 x2_out, xn2_out, q_out):
    y = y_ref[...]
    hsum = hsum_ref[...]
    mean = _dot_hl(y, hsum) * (1.0 / HEAD_DIM)
    d = y - mean
    var = _dot_hl(d * d, hsum) * (1.0 / HEAD_DIM)
    yn = d * lax.rsqrt(var + GN_EPS) * gnw_ref[...] + gnb_ref[...] + bonus_ref[...]
    y_b = yn * g_ref[...]
    merged = ma_ref[...] + sgb_ref[...] * _dot(y_b, w_pb_ref[...])
    x2 = x_ref[...] + _dot(merged, w_o_ref[...])
    x2_out[...] = x2
    xn2 = _rms_norm(x2, n2g_ref[...])
    xn2_out[...] = xn2
    hi, lo = _split(xn2)
    wq_hi = wq_hi_ref[...]
    q_out[...] = (jnp.dot(hi, wq_hi, preferred_element_type=F32)
                  + jnp.dot(lo, wq_hi, preferred_element_type=F32)
                  + jnp.dot(hi, wq_lo_ref[...], preferred_element_type=F32))


def _post_call(y, bonus, g, ma, sgb, x, w, tt):
    n = y.shape[0]
    row = lambda i: (i, 0)
    const = lambda i: (0, 0)
    tok = lambda c: pl.BlockSpec((tt, c), row)
    weights = (w["gn_w"], w["gn_b"], w["hsum"], w["w_pb"], w["w_o"], w["norm2_g"], w["wq_hi"], w["wq_lo"])
    d_q = w["wq_hi"].shape[1]
    return pl.pallas_call(
        _post_kernel, grid=(n // tt,),
        in_specs=[tok(D_RWKV)] * 3 + [tok(D_MODEL)] * 3 + [pl.BlockSpec(a.shape, const) for a in weights],
        out_specs=[tok(D_MODEL), tok(D_MODEL), tok(d_q)],
        out_shape=[jax.ShapeDtypeStruct((n, D_MODEL), F32)] * 2 + [jax.ShapeDtypeStruct((n, d_q), F32)],
        compiler_params=pltpu.CompilerParams(dimension_semantics=("arbitrary",),
                                             vmem_limit_bytes=VMEM_LIMIT_BYTES),
        name="post",
    )(y, bonus, g, ma, sgb, x, *weights)


def _top_rows(s, payload=None):
    n_rows, n_cols = s.shape
    iota = lax.broadcasted_iota(jnp.int32, s.shape, 0)
    slot = lax.broadcasted_iota(jnp.int32, (PEER_TOPK, n_cols), 0)

    def body(j, carry):
        s, vals, idxs = carry
        m = jnp.max(s, axis=0, keepdims=True)
        idx = jnp.min(jnp.where(s == m, iota, n_rows), axis=0, keepdims=True)
        hit = iota == idx
        out = idx if payload is None else jnp.max(jnp.where(hit, payload, -1), axis=0, keepdims=True)
        vals = jnp.where(slot == j, m, vals)
        idxs = jnp.where(slot == j, out, idxs)
        return jnp.where(hit, -jnp.inf, s), vals, idxs

    init = (s, jnp.zeros((PEER_TOPK, n_cols), F32), jnp.zeros((PEER_TOPK, n_cols), jnp.int32))
    _, vals, idxs = lax.fori_loop(0, PEER_TOPK, body, init)
    return vals, idxs


def _topk_kernel(q_ref, khi_ref, klo_ref, e_out, g_out):
    tops = []
    for p in range(2):
        q_hi, q_lo = _split(q_ref[:, p * PEER_HALF:(p + 1) * PEER_HALF])
        k_hi = khi_ref[0, p]
        nt = lambda x, y: lax.dot_general(x, y, (((1,), (1,)), ((), ())), preferred_element_type=F32)
        s = nt(k_hi, q_hi) + nt(k_hi, q_lo) + nt(klo_ref[0, p], q_hi)
        tops.append(_top_rows(s))
    (v1, i1), (v2, i2) = tops
    cand = jnp.concatenate([v1[i:i + 1] + v2 for i in range(PEER_TOPK)], axis=0)
    cidx = jnp.concatenate([i1[i:i + 1] * PEER_KEYS + i2 for i in range(PEER_TOPK)], axis=0)
    sc, eidx = _top_rows(cand, cidx)
    e = jnp.exp(sc - sc[0:1])
    e_out[...] = eidx
    g_out[...] = e / jnp.sum(e, axis=0, keepdims=True)


def _topk_call(q, khi, klo):
    n = q.shape[0]
    sel = pl.BlockSpec((PEER_TOPK, TOK_TILE), lambda i, h: (h, i))
    keys = pl.BlockSpec((1, 2, PEER_KEYS, PEER_HALF), lambda i, h: (h, 0, 0, 0))
    return pl.pallas_call(
        _topk_kernel, grid=(n // TOK_TILE, PEER_HEADS),
        in_specs=[pl.BlockSpec((TOK_TILE, 2 * PEER_HALF), lambda i, h: (i, h)), keys, keys],
        out_specs=[sel, sel],
        out_shape=[jax.ShapeDtypeStruct((PEER_SEL, n), jnp.int32), jax.ShapeDtypeStruct((PEER_SEL, n), F32)],
        compiler_params=pltpu.CompilerParams(dimension_semantics=("arbitrary", "arbitrary")),
        name="topk",
    )(q, khi, klo)


def _peer_kernel(e_ref, g_ref, xn2_ref, x2_ref, nfg_ref, u_hbm, v_hbm, y_out,
                 e_vmem, e_smem, g_vmem, ubuf, vbuf, sem_e, sem_u, sem_v):
    e_vmem[...] = e_ref[...].T
    g_vmem[...] = g_ref[...].T
    to_smem = pltpu.make_async_copy(e_vmem, e_smem, sem_e)
    to_smem.start()
    to_smem.wait()

    def row_copies(t, slot, j):
        idx = e_smem[t, j]
        return (pltpu.make_async_copy(u_hbm.at[pl.ds(idx, 1), :], ubuf.at[slot, pl.ds(j, 1), :],
                                      sem_u.at[slot]),
                pltpu.make_async_copy(v_hbm.at[pl.ds(idx, 1), :], vbuf.at[slot, pl.ds(j, 1), :],
                                      sem_v.at[slot]))

    def start_token(t, slot):
        for j in range(PEER_SEL):
            cu, cv = row_copies(t, slot, j)
            cu.start()
            cv.start()

    def wait_token(t, slot):
        for j in range(PEER_SEL):
            cu, cv = row_copies(t, slot, j)
            cu.wait()
            cv.wait()

    start_token(0, 0)

    def body(t, carry):
        slot = lax.rem(t, 2)

        @pl.when(t + 1 < TOK_TILE)
        def _():
            start_token(t + 1, 1 - slot)

        wait_token(t, slot)
        x = jnp.broadcast_to(xn2_ref[pl.ds(t, 1), :], (8, D_MODEL))
        act = _dot_nt(x, ubuf[slot])
        gelu = 0.5 * act * (1.0 + jnp.tanh(0.7978845608028654 * (act + 0.044715 * (act * act * act))))
        coef = g_vmem[pl.ds(t, 1), :] * gelu
        mix = _dot(coef, vbuf[slot])
        out = x2_ref[pl.ds(t, 1), :] + mix[0:1]
        y_out[pl.ds(t, 1), :] = _rms_norm(out, nfg_ref[...])
        return carry

    lax.fori_loop(0, TOK_TILE, body, 0)


def _peer_call(eidx, gate, xn2, x2, nfg, u, v):
    n = xn2.shape[0]
    sel = pl.BlockSpec((PEER_SEL, TOK_TILE), lambda i: (0, i))
    tok = pl.BlockSpec((TOK_TILE, D_MODEL), lambda i: (i, 0))
    return pl.pallas_call(
        _peer_kernel, grid=(n // TOK_TILE,),
        in_specs=[sel, sel, tok, tok, pl.BlockSpec((1, D_MODEL), lambda i: (0, 0)),
                  pl.BlockSpec(memory_space=pl.ANY), pl.BlockSpec(memory_space=pl.ANY)],
        out_specs=tok,
        out_shape=jax.ShapeDtypeStruct((n, D_MODEL), F32),
        scratch_shapes=[pltpu.VMEM((TOK_TILE, PEER_SEL), jnp.int32),
                        pltpu.SMEM((TOK_TILE, PEER_SEL), jnp.int32),
                        pltpu.VMEM((TOK_TILE, PEER_SEL), F32),
                        pltpu.VMEM((2, PEER_SEL, D_MODEL), F32),
                        pltpu.VMEM((2, PEER_SEL, D_MODEL), F32),
                        pltpu.SemaphoreType.DMA,
                        pltpu.SemaphoreType.DMA((2,)),
                        pltpu.SemaphoreType.DMA((2,))],
        compiler_params=pltpu.CompilerParams(dimension_semantics=("arbitrary",)),
        name="peer",
    )(eidx, gate, xn2, x2, nfg, u, v)


def _tile_choices(bsz, seq):
    tt = min(seq, 256)
    chunk = min(seq, 64)
    post = min(bsz * seq, 256)
    return tt, chunk, post


def _run_trunk(x, st_conv, st_shift, st_wkv, w):
    bsz, seq, _ = x.shape
    n = bsz * seq
    tt, chunk, post_tt = _tile_choices(bsz, seq)
    (r, lw, k, v, a, b, bonus, g, ma, sgb, new_shift, new_conv) = _pre_call(
        x, st_conv, st_shift.reshape(bsz, 1, D_MODEL), w, tt)
    y, new_wkv = _wkv_call(r, lw, k, v, a, b, st_wkv, chunk)
    flat = lambda t: t.reshape(n, t.shape[-1])
    x2, xn2, q = _post_call(flat(y), flat(bonus), flat(g), flat(ma), flat(sgb), flat(x), w, post_tt)
    eidx, gate = _topk_call(q, w["keys_hi"], w["keys_lo"])
    out = _peer_call(eidx, gate, xn2, x2, w["norm_f_g"], w["peer_u"], w["peer_v"])
    return (out.reshape(bsz, seq, D_MODEL), new_conv[None], new_shift.reshape(1, bsz, D_MODEL),
            new_wkv[None])


def kernel(x_prompt, x_sample, state_conv, state_shift, state_wkv, norm1_g, w_in, conv_w, mu_rkv, mu_wag,
           w0, w1, w2, a0, a1, a2, g1, g2, k_k, k_a, r_k, gn_w, gn_b, w_pa, w_pb, w_o, norm2_g,
           peer_wq, peer_keys, peer_u, peer_v, norm_f_g):
    row = lambda t: t.reshape(1, -1)
    head = jnp.arange(D_RWKV) // HEAD_DIM
    wq_hi, wq_lo = _split(peer_wq[0])
    keys = peer_keys[0]
    keys_hi, keys_lo = _split(keys)
    w = dict(
        norm1_g=norm1_g, w_in=w_in[0].astype(BF16), conv_w=conv_w[0], mu_rkv=mu_rkv, mu_wag=mu_wag[0],
        w0=w0, w1=w1[0].astype(BF16), w2=w2[0].astype(BF16), a0=a0, a1=a1[0].astype(BF16),
        a2=a2[0].astype(BF16), g1=g1[0].astype(BF16), g2=g2[0].astype(BF16), k_k=k_k, k_a=k_a,
        r_k=row(r_k[0]), gn_w=gn_w, gn_b=gn_b, w_pa=w_pa[0].astype(BF16), w_pb=w_pb[0].astype(BF16),
        w_o=w_o[0].astype(BF16), norm2_g=norm2_g, wq_hi=wq_hi, wq_lo=wq_lo, keys_hi=keys_hi,
        keys_lo=keys_lo, peer_u=peer_u[0], peer_v=peer_v[0], norm_f_g=row(norm_f_g),
        hsum=(head[:, None] == head[None, :]).astype(BF16),
    )
    bp = x_prompt.shape[0]
    zero_conv = jnp.zeros((bp, CONV_W - 1, D_CONV), F32)
    zero_shift = jnp.zeros((bp, D_MODEL), F32)
    zero_wkv = jnp.zeros((bp, HEADS, HEAD_DIM, HEAD_DIM), F32)
    y_p, conv_p, shift_p, wkv_p = _run_trunk(x_prompt, zero_conv, zero_shift, zero_wkv, w)
    y_s, conv_s, shift_s, wkv_s = _run_trunk(x_sample, state_conv[0], state_shift[0], state_wkv[0], w)
    return (y_p, y_s, conv_p, shift_p, wkv_p, conv_s, shift_s, wkv_s)
```

```python
import functools

import jax
import jax.numpy as jnp
from jax import lax
from jax.experimental import pallas as pl
from jax.experimental.pallas import tpu as pltpu

F32 = jnp.float32
BF16 = jnp.bfloat16

D_MODEL = 1024
D_CONV = 512
CONV_W = 3
HEADS = 8
HEAD_DIM = 64
D_RWKV = HEADS * HEAD_DIM
PAIR = 2 * HEAD_DIM
N_PAIRS = HEADS // 2
GN_EPS = 64e-5
RMS_EPS = 1e-6
OFF_RKV = 3 * D_CONV
OFF_GATE = OFF_RKV + 3 * D_RWKV
D_IN = OFF_GATE + 2 * D_MODEL

PEER_HEADS = 8
PEER_KEYS = 128
PEER_HALF = 128
PEER_TOPK = 16
PEER_SEL = PEER_HEADS * PEER_TOPK
TOK_TILE = 128

VMEM_LIMIT_BYTES = 56 * 1024 * 1024


def _dot(a, b):
    return jnp.dot(a.astype(BF16), b.astype(BF16), preferred_element_type=F32)


def _dot_nt(a, b):
    return lax.dot_general(a.astype(BF16), b.astype(BF16), (((1,), (1,)), ((), ())),
                           preferred_element_type=F32)


def _dot_tn(a, b):
    return lax.dot_general(a.astype(BF16), b.astype(BF16), (((0,), (0,)), ((), ())),
                           preferred_element_type=F32)


def _split(a):
    hi = a.astype(BF16)
    lo = (a - hi.astype(F32)).astype(BF16)
    return hi, lo


def _dot_hl(a, w_bf16):
    hi, lo = _split(a)
    return (jnp.dot(hi, w_bf16, preferred_element_type=F32)
            + jnp.dot(lo, w_bf16, preferred_element_type=F32))


def _sigmoid(x):
    return 1.0 / (1.0 + jnp.exp(-x))


def _rms_norm(x, g):
    return x * lax.rsqrt(jnp.mean(x * x, axis=-1, keepdims=True) + RMS_EPS) * g


def _shift_rows(a, carry, n):
    rolled = pltpu.roll(a, n, 0)
    row = lax.broadcasted_iota(jnp.int32, a.shape, 0)
    for i in range(n):
        rolled = jnp.where(row == i, carry[i:i + 1], rolled)
    return rolled


def _pre_kernel(x_ref, conv0_ref, shift0_ref, n1g_ref, w_in_ref, convw_ref, mu_rkv_ref, mu_wag_ref,
                w0_ref, w1_ref, w2_ref, a0_ref, a1_ref, a2_ref, g1_ref, g2_ref, kk_ref, ka_ref, rk_ref,
                w_pa_ref, hsum_ref,
                r_out, lw_out, k_out, v_out, a_out, b_out, bonus_out, g_out, ma_out, sgb_out,
                nshift_out, nconv_out,
                xn_c, zrkv_c, u_c):
    t = pl.program_id(1)
    tt = x_ref.shape[1]
    xn = _rms_norm(x_ref[0], n1g_ref[...])
    xnb = xn.astype(BF16)

    @pl.when(t == 0)
    def _():
        prev = jnp.broadcast_to(shift0_ref[0], (8, D_MODEL))
        xn_c[...] = prev
        zrkv_c[...] = jnp.dot(prev.astype(BF16), w_in_ref[:, OFF_RKV:OFF_GATE],
                              preferred_element_type=F32)
        u_c[0:2, :] = conv0_ref[0]

    zbch = jnp.dot(xnb, w_in_ref[:, 0:OFF_RKV], preferred_element_type=F32)
    zb = zbch[:, 0:D_CONV]
    u = zbch[:, D_CONV:2 * D_CONV] * zbch[:, 2 * D_CONV:3 * D_CONV]
    u_prev = u_c[0:2, :]
    u1 = _shift_rows(u, u_prev[1:2], 1)
    u2 = _shift_rows(u, u_prev, 2)
    cw = convw_ref[...]
    y_a = zb * (cw[0:1] * u2 + cw[1:2] * u1 + cw[2:3] * u)
    u_last = u[tt - 2:tt, :]
    nconv_out[0] = u_last
    u_c[0:2, :] = u_last

    zg = jnp.dot(xnb, w_in_ref[:, OFF_GATE:D_IN], preferred_element_type=F32)
    ma_out[0] = _sigmoid(zg[:, 0:D_MODEL]) * _dot(y_a, w_pa_ref[...])
    sgb_out[0] = _sigmoid(zg[:, D_MODEL:2 * D_MODEL])

    zrkv = jnp.dot(xnb, w_in_ref[:, OFF_RKV:OFF_GATE], preferred_element_type=F32)
    zprev = _shift_rows(zrkv, zrkv_c[0:1, :], 1)
    zs = zrkv + mu_rkv_ref[...] * (zprev - zrkv)
    xprev = _shift_rows(xn, xn_c[0:1, :], 1)
    dx = xprev - xn
    mu = mu_wag_ref[...]
    xw = xn + dx * mu[0:1]
    xa = xn + dx * mu[1:2]
    xg = xn + dx * mu[2:3]
    xn_last = xn[tt - 1:tt, :]
    nshift_out[0] = xn_last
    xn_c[0:1, :] = xn_last
    zrkv_c[0:1, :] = zrkv[tt - 1:tt, :]

    wl = w0_ref[...] + _dot(jnp.tanh(_dot(xw, w1_ref[...])), w2_ref[...])
    softplus = jnp.maximum(-wl, 0.0) + jnp.log(1.0 + jnp.exp(-jnp.abs(wl)))
    lw_out[0] = -jnp.exp(-softplus - 0.5)
    a_sig = _sigmoid(a0_ref[...] + _dot(_dot(xa, a1_ref[...]), a2_ref[...]))
    g_out[0] = _dot(_sigmoid(_dot(xg, g1_ref[...])), g2_ref[...])

    r = zs[:, 0:D_RWKV]
    k = zs[:, D_RWKV:2 * D_RWKV]
    v = zs[:, 2 * D_RWKV:3 * D_RWKV]
    hsum = hsum_ref[...]
    kk = k * kk_ref[...]
    kk = kk / jnp.maximum(jnp.sqrt(_dot_hl(kk * kk, hsum)), 1e-12)
    k = k * (1.0 + (a_sig - 1.0) * ka_ref[...])
    r_out[0] = r
    k_out[0] = k
    v_out[0] = v
    a_out[0] = -kk
    b_out[0] = kk * a_sig
    bonus_out[0] = _dot_hl(r * k * rk_ref[...], hsum) * v


def _pre_call(x, conv0, shift0, w, tt):
    bsz, seq, _ = x.shape
    grid = (bsz, seq // tt)
    row = lambda b, t: (b, t, 0)
    per_b = lambda b, t: (b, 0, 0)
    const2 = lambda b, t: (0, 0)

    def tok(c):
        return pl.BlockSpec((1, tt, c), row)

    def full(a):
        return pl.BlockSpec(a.shape, const2)

    weights = (w["norm1_g"], w["w_in"], w["conv_w"], w["mu_rkv"], w["mu_wag"], w["w0"], w["w1"], w["w2"],
               w["a0"], w["a1"], w["a2"], w["g1"], w["g2"], w["k_k"], w["k_a"], w["r_k"], w["w_pa"],
               w["hsum"])
    in_specs = [tok(D_MODEL), pl.BlockSpec((1, CONV_W - 1, D_CONV), per_b),
                pl.BlockSpec((1, 1, D_MODEL), per_b)] + [full(a) for a in weights]
    tok_shape = lambda c: jax.ShapeDtypeStruct((bsz, seq, c), F32)
    out_shape = [tok_shape(D_RWKV)] * 8 + [tok_shape(D_MODEL)] * 2 + [
        jax.ShapeDtypeStruct((bsz, 1, D_MODEL), F32),
        jax.ShapeDtypeStruct((bsz, CONV_W - 1, D_CONV), F32)]
    out_specs = [tok(D_RWKV)] * 8 + [tok(D_MODEL)] * 2 + [
        pl.BlockSpec((1, 1, D_MODEL), per_b), pl.BlockSpec((1, CONV_W - 1, D_CONV), per_b)]
    return pl.pallas_call(
        _pre_kernel, grid=grid, in_specs=in_specs, out_specs=out_specs, out_shape=out_shape,
        scratch_shapes=[pltpu.VMEM((8, D_MODEL), F32), pltpu.VMEM((8, 3 * D_RWKV), F32),
                        pltpu.VMEM((8, D_CONV), F32)],
        compiler_params=pltpu.CompilerParams(dimension_semantics=("arbitrary", "arbitrary"),
                                             vmem_limit_bytes=VMEM_LIMIT_BYTES),
        name="pre",
    )(x, conv0, shift0, *weights)


def _wkv_kernel(r_ref, lw_ref, k_ref, v_ref, a_ref, b_ref, s0_ref, tri_ref,
                y_out, s_out, s_c):
    c = pl.program_id(2)
    L = r_ref.shape[1]
    lane = lax.broadcasted_iota(jnp.int32, (L, PAIR), 1)
    first = lane < HEAD_DIM
    s_row = lax.broadcasted_iota(jnp.int32, (PAIR, PAIR), 0)
    s_col = lax.broadcasted_iota(jnp.int32, (PAIR, PAIR), 1)
    s_mask = (s_row < HEAD_DIM) == (s_col < HEAD_DIM)

    @pl.when(c == 0)
    def _():
        s0 = s0_ref[0]
        z = jnp.zeros((HEAD_DIM, HEAD_DIM), F32)
        s_c[...] = jnp.concatenate([jnp.concatenate([s0[0], z], axis=1),
                                    jnp.concatenate([z, s0[1]], axis=1)], axis=0)

    S = s_c[...]
    r = r_ref[0]
    lw = lw_ref[0]
    k = k_ref[0]
    v = v_ref[0]
    a = a_ref[0]
    b = b_ref[0]

    l1 = lw.astype(BF16)
    r1 = lw - l1.astype(F32)
    l2 = r1.astype(BF16)
    l3 = (r1 - l2.astype(F32)).astype(BF16)
    tri = tri_ref[...]
    cum = (jnp.dot(tri, l1, preferred_element_type=F32) + jnp.dot(tri, l2, preferred_element_type=F32)
           + jnp.dot(tri, l3, preferred_element_type=F32))
    cum_l = cum[L - 1:L, :]
    w_inc = jnp.exp(cum)
    w_inv = jnp.exp(-cum)
    at = a * jnp.exp(cum - lw)
    bt = b * w_inv
    kt = k * w_inv
    rt = r * w_inc
    dec = jnp.exp(cum_l - cum)

    row = lax.broadcasted_iota(jnp.int32, (L, L), 0)
    col = lax.broadcasted_iota(jnp.int32, (L, L), 1)
    strict = row > col
    incl = row >= col

    def per_head(x):
        return jnp.where(first, x, 0.0), jnp.where(first, 0.0, x)

    def merge(x1, x2):
        return jnp.where(first, x1, x2)

    at1, at2 = per_head(at)
    rt1, rt2 = per_head(rt)
    zero = jnp.zeros((L, L), F32)
    mab = [jnp.where(strict, _dot_nt(x, bt), zero) for x in (at1, at2)]
    mak = [jnp.where(strict, _dot_nt(x, kt), zero) for x in (at1, at2)]
    nrb = [jnp.where(incl, _dot_nt(x, bt), zero) for x in (rt1, rt2)]
    nrk = [jnp.where(incl, _dot_nt(x, kt), zero) for x in (rt1, rt2)]

    U = _dot_nt(at, S) + merge(_dot(mak[0], v), _dot(mak[1], v))
    n = 1
    while n < L:
        U = U + merge(_dot(mab[0], U), _dot(mab[1], U))
        n *= 2
        if n < L:
            mab = [_dot(m, m) for m in mab]
    y = (_dot_nt(rt, S) + merge(_dot(nrb[0], U), _dot(nrb[1], U))
         + merge(_dot(nrk[0], v), _dot(nrk[1], v)))
    y_out[0] = y

    s_new = S * jnp.exp(cum_l) + jnp.where(s_mask, _dot_tn(U, b * dec) + _dot_tn(v, k * dec), 0.0)
    s_c[...] = s_new
    s_out[0, 0] = s_new[0:HEAD_DIM, 0:HEAD_DIM]
    s_out[0, 1] = s_new[HEAD_DIM:PAIR, HEAD_DIM:PAIR]


def _wkv_call(r, lw, k, v, a, b, s0, chunk):
    bsz, seq, _ = r.shape
    grid = (bsz, N_PAIRS, seq // chunk)
    tok = pl.BlockSpec((1, chunk, PAIR), lambda bi, p, c: (bi, c, p))
    st = pl.BlockSpec((1, 2, HEAD_DIM, HEAD_DIM), lambda bi, p, c: (bi, p, 0, 0))
    tri = (jnp.arange(chunk)[:, None] >= jnp.arange(chunk)[None, :]).astype(BF16)
    return pl.pallas_call(
        _wkv_kernel, grid=grid,
        in_specs=[tok] * 6 + [st, pl.BlockSpec((chunk, chunk), lambda bi, p, c: (0, 0))],
        out_specs=[tok, st],
        out_shape=[jax.ShapeDtypeStruct((bsz, seq, D_RWKV), F32),
                   jax.ShapeDtypeStruct((bsz, HEADS, HEAD_DIM, HEAD_DIM), F32)],
        scratch_shapes=[pltpu.VMEM((PAIR, PAIR), F32)],
        compiler_params=pltpu.CompilerParams(
            dimension_semantics=("arbitrary", "arbitrary", "arbitrary")),
        name="wkv",
    )(r, lw, k, v, a, b, s0, tri)


def _post_kernel(y_ref, bonus_ref, g_ref, ma_ref, sgb_ref, x_ref, gnw_ref, gnb_ref, hsum_ref,
                 w_pb_ref, w_o_ref, n2g_ref, wq_hi_ref, wq_lo_ref,
                 x2_out, xn2_out, q_out):
    y = y_ref[...]
    hsum = hsum_ref[...]
    mean = _dot_hl(y, hsum) * (1.0 / HEAD_DIM)
    d = y - mean
    var = _dot_hl(d * d, hsum) * (1.0 / HEAD_DIM)
    yn = d * lax.rsqrt(var + GN_EPS) * gnw_ref[...] + gnb_ref[...] + bonus_ref[...]
    y_b = yn * g_ref[...]
    merged = ma_ref[...] + sgb_ref[...] * _dot(y_b, w_pb_ref[...])
    x2 = x_ref[...] + _dot(merged, w_o_ref[...])
    x2_out[...] = x2
    xn2 = _rms_norm(x2, n2g_ref[...])
    xn2_out[...] = xn2
    hi, lo = _split(xn2)
    wq_hi = wq_hi_ref[...]
    q_out[...] = (jnp.dot(hi, wq_hi, preferred_element_type=F32)
                  + jnp.dot(lo, wq_hi, preferred_element_type=F32)
                  + jnp.dot(hi, wq_lo_ref[...], preferred_element_type=F32))


def _post_call(y, bonus, g, ma, sgb, x, w, tt):
    n = y.shape[0]
    row = lambda i: (i, 0)
    const = lambda i: (0, 0)
    tok = lambda c: pl.BlockSpec((tt, c), row)
    weights = (w["gn_w"], w["gn_b"], w["hsum"], w["w_pb"], w["w_o"], w["norm2_g"], w["wq_hi"], w["wq_lo"])
    d_q = w["wq_hi"].shape[1]
    return pl.pallas_call(
        _post_kernel, grid=(n // tt,),
        in_specs=[tok(D_RWKV)] * 3 + [tok(D_MODEL)] * 3 + [pl.BlockSpec(a.shape, const) for a in weights],
        out_specs=[tok(D_MODEL), tok(D_MODEL), tok(d_q)],
        out_shape=[jax.ShapeDtypeStruct((n, D_MODEL), F32)] * 2 + [jax.ShapeDtypeStruct((n, d_q), F32)],
        compiler_params=pltpu.CompilerParams(dimension_semantics=("arbitrary",),
                                             vmem_limit_bytes=VMEM_LIMIT_BYTES),
        name="post",
    )(y, bonus, g, ma, sgb, x, *weights)


def _top_rows(s, payload=None):
    n_rows, n_cols = s.shape
    iota = lax.broadcasted_iota(jnp.int32, s.shape, 0)
    slot = lax.broadcasted_iota(jnp.int32, (PEER_TOPK, n_cols), 0)

    def body(j, carry):
        s, vals, idxs = carry
        m = jnp.max(s, axis=0, keepdims=True)
        idx = jnp.min(jnp.where(s == m, iota, n_rows), axis=0, keepdims=True)
        hit = iota == idx
        out = idx if payload is None else jnp.max(jnp.where(hit, payload, -1), axis=0, keepdims=True)
        vals = jnp.where(slot == j, m, vals)
        idxs = jnp.where(slot == j, out, idxs)
        return jnp.where(hit, -jnp.inf, s), vals, idxs

    init = (s, jnp.zeros((PEER_TOPK, n_cols), F32), jnp.zeros((PEER_TOPK, n_cols), jnp.int32))
    _, vals, idxs = lax.fori_loop(0, PEER_TOPK, body, init)
    return vals, idxs


def _topk_kernel(q_ref, khi_ref, klo_ref, e_out, g_out):
    tops = []
    for p in range(2):
        q_hi, q_lo = _split(q_ref[:, p * PEER_HALF:(p + 1) * PEER_HALF])
        k_hi = khi_ref[0, p]
        nt = lambda x, y: lax.dot_general(x, y, (((1,), (1,)), ((), ())), preferred_element_type=F32)
        s = nt(k_hi, q_hi) + nt(k_hi, q_lo) + nt(klo_ref[0, p], q_hi)
        tops.append(_top_rows(s))
    (v1, i1), (v2, i2) = tops
    cand = jnp.concatenate([v1[i:i + 1] + v2 for i in range(PEER_TOPK)], axis=0)
    cidx = jnp.concatenate([i1[i:i + 1] * PEER_KEYS + i2 for i in range(PEER_TOPK)], axis=0)
    sc, eidx = _top_rows(cand, cidx)
    e = jnp.exp(sc - sc[0:1])
    e_out[...] = eidx
    g_out[...] = e / jnp.sum(e, axis=0, keepdims=True)


def _topk_call(q, khi, klo):
    n = q.shape[0]
    sel = pl.BlockSpec((PEER_TOPK, TOK_TILE), lambda i, h: (h, i))
    keys = pl.BlockSpec((1, 2, PEER_KEYS, PEER_HALF), lambda i, h: (h, 0, 0, 0))
    return pl.pallas_call(
        _topk_kernel, grid=(n // TOK_TILE, PEER_HEADS),
        in_specs=[pl.BlockSpec((TOK_TILE, 2 * PEER_HALF), lambda i, h: (i, h)), keys, keys],
        out_specs=[sel, sel],
        out_shape=[jax.ShapeDtypeStruct((PEER_SEL, n), jnp.int32), jax.ShapeDtypeStruct((PEER_SEL, n), F32)],
        compiler_params=pltpu.CompilerParams(dimension_semantics=("arbitrary", "arbitrary")),
        name="topk",
    )(q, khi, klo)


def _pack_kernel(u_ref, v_ref, w_out):
    ub = lax.bitcast_convert_type(u_ref[...].astype(BF16).astype(F32), jnp.uint32)
    vb = lax.bitcast_convert_type(v_ref[...].astype(BF16).astype(F32), jnp.uint32)
    w_out[...] = (ub & jnp.uint32(0xFFFF0000)) | (vb >> 16)


def _pack_call(u, v):
    n, d = u.shape
    rows = 512
    blk = pl.BlockSpec((rows, d), lambda i: (i, 0))
    return pl.pallas_call(
        _pack_kernel, grid=(n // rows,), in_specs=[blk, blk], out_specs=blk,
        out_shape=jax.ShapeDtypeStruct((n, d), jnp.uint32),
        compiler_params=pltpu.CompilerParams(dimension_semantics=("arbitrary",)),
        name="pack",
    )(u, v)


def _peer_kernel(e_ref, g_ref, xn2_ref, x2_ref, nfg_ref, w_hbm, y_out,
                 e_vmem, e_smem, g_vmem, wbuf, sem_e, sem_w):
    e_vmem[...] = e_ref[...].T
    g_vmem[...] = g_ref[...].T
    to_smem = pltpu.make_async_copy(e_vmem, e_smem, sem_e)
    to_smem.start()
    to_smem.wait()

    def row_copy(t, slot, j):
        return pltpu.make_async_copy(w_hbm.at[e_smem[t, j]], wbuf.at[slot, :, j, :], sem_w.at[slot])

    def start_token(t, slot):
        for j in range(PEER_SEL):
            row_copy(t, slot, j).start(priority=j % 2)

    def wait_token(t, slot):
        for j in range(PEER_SEL):
            row_copy(t, slot, j).wait()

    def mix_token(t, slot):
        w = wbuf[slot]
        planes_u = [lax.bitcast_convert_type(w[s] & jnp.uint32(0xFFFF0000), F32) for s in range(8)]
        planes_v = [lax.bitcast_convert_type(w[s] << 16, F32) for s in range(8)]
        ug = jnp.concatenate(planes_u, axis=1)
        vg = jnp.concatenate(planes_v, axis=1)
        x = jnp.broadcast_to(xn2_ref[pl.ds(t, 1), :], (8, D_MODEL))
        act = _dot_nt(x, ug)
        gelu = 0.5 * act * (1.0 + jnp.tanh(0.7978845608028654 * (act + 0.044715 * (act * act * act))))
        coef = g_vmem[pl.ds(t, 1), :] * gelu
        mix = _dot(coef, vg)
        out = x2_ref[pl.ds(t, 1), :] + mix[0:1]
        y_out[pl.ds(t, 1), :] = _rms_norm(out, nfg_ref[...])

    start_token(0, 0)

    def body(i, carry):
        t = 2 * i
        start_token(t + 1, 1)
        wait_token(t, 0)
        mix_token(t, 0)

        @pl.when(t + 2 < TOK_TILE)
        def _():
            start_token(t + 2, 0)

        wait_token(t + 1, 1)
        mix_token(t + 1, 1)
        return carry

    lax.fori_loop(0, TOK_TILE // 2, body, 0)


def _peer_call(eidx, gate, xn2, x2, nfg, w3):
    n = xn2.shape[0]
    sel = pl.BlockSpec((PEER_SEL, TOK_TILE), lambda i: (0, i))
    tok = pl.BlockSpec((TOK_TILE, D_MODEL), lambda i: (i, 0))
    return pl.pallas_call(
        _peer_kernel, grid=(n // TOK_TILE,),
        in_specs=[sel, sel, tok, tok, pl.BlockSpec((1, D_MODEL), lambda i: (0, 0)),
                  pl.BlockSpec(memory_space=pl.ANY)],
        out_specs=tok,
        out_shape=jax.ShapeDtypeStruct((n, D_MODEL), F32),
        scratch_shapes=[pltpu.VMEM((TOK_TILE, PEER_SEL), jnp.int32),
                        pltpu.SMEM((TOK_TILE, PEER_SEL), jnp.int32),
                        pltpu.VMEM((TOK_TILE, PEER_SEL), F32),
                        pltpu.VMEM((2, 8, PEER_SEL, 128), jnp.uint32),
                        pltpu.SemaphoreType.DMA,
                        pltpu.SemaphoreType.DMA((2,))],
        compiler_params=pltpu.CompilerParams(dimension_semantics=("arbitrary",)),
        name="peer",
    )(eidx, gate, xn2, x2, nfg, w3)


def _tile_choices(bsz, seq):
    tt = min(seq, 256)
    chunk = min(seq, 64)
    post = min(bsz * seq, 256)
    return tt, chunk, post


def _run_trunk(x, st_conv, st_shift, st_wkv, w):
    bsz, seq, _ = x.shape
    n = bsz * seq
    tt, chunk, post_tt = _tile_choices(bsz, seq)
    (r, lw, k, v, a, b, bonus, g, ma, sgb, new_shift, new_conv) = _pre_call(
        x, st_conv, st_shift.reshape(bsz, 1, D_MODEL), w, tt)
    y, new_wkv = _wkv_call(r, lw, k, v, a, b, st_wkv, chunk)
    flat = lambda t: t.reshape(n, t.shape[-1])
    x2, xn2, q = _post_call(flat(y), flat(bonus), flat(g), flat(ma), flat(sgb), flat(x), w, post_tt)
    eidx, gate = _topk_call(q, w["keys_hi"], w["keys_lo"])
    out = _peer_call(eidx, gate, xn2, x2, w["norm_f_g"], w["peer_w"])
    return (out.reshape(bsz, seq, D_MODEL), new_conv[None], new_shift.reshape(1, bsz, D_MODEL),
            new_wkv[None])


def kernel(x_prompt, x_sample, state_conv, state_shift, state_wkv, norm1_g, w_in, conv_w, mu_rkv, mu_wag,
           w0, w1, w2, a0, a1, a2, g1, g2, k_k, k_a, r_k, gn_w, gn_b, w_pa, w_pb, w_o, norm2_g,
           peer_wq, peer_keys, peer_u, peer_v, norm_f_g):
    row = lambda t: t.reshape(1, -1)
    head = jnp.arange(D_RWKV) // HEAD_DIM
    wq_hi, wq_lo = _split(peer_wq[0])
    keys = peer_keys[0]
    keys_hi, keys_lo = _split(keys)
    w = dict(
        norm1_g=norm1_g, w_in=w_in[0].astype(BF16), conv_w=conv_w[0], mu_rkv=mu_rkv, mu_wag=mu_wag[0],
        w0=w0, w1=w1[0].astype(BF16), w2=w2[0].astype(BF16), a0=a0, a1=a1[0].astype(BF16),
        a2=a2[0].astype(BF16), g1=g1[0].astype(BF16), g2=g2[0].astype(BF16), k_k=k_k, k_a=k_a,
        r_k=row(r_k[0]), gn_w=gn_w, gn_b=gn_b, w_pa=w_pa[0].astype(BF16), w_pb=w_pb[0].astype(BF16),
        w_o=w_o[0].astype(BF16), norm2_g=norm2_g, wq_hi=wq_hi, wq_lo=wq_lo, keys_hi=keys_hi,
        keys_lo=keys_lo, norm_f_g=row(norm_f_g),
        peer_w=_pack_call(peer_u[0], peer_v[0]).reshape(-1, 8, 128),
        hsum=(head[:, None] == head[None, :]).astype(BF16),
    )
    bp = x_prompt.shape[0]
    zero_conv = jnp.zeros((bp, CONV_W - 1, D_CONV), F32)
    zero_shift = jnp.zeros((bp, D_MODEL), F32)
    zero_wkv = jnp.zeros((bp, HEADS, HEAD_DIM, HEAD_DIM), F32)
    y_s, conv_s, shift_s, wkv_s = _run_trunk(x_sample, state_conv[0], state_shift[0], state_wkv[0], w)
    y_p, conv_p, shift_p, wkv_p = _run_trunk(x_prompt, zero_conv, zero_shift, zero_wkv, w)
    return (y_p, y_s, conv_p, shift_p, wkv_p, conv_s, shift_s, wkv_s)
```

```python
import functools

import jax
import jax.numpy as jnp
from jax import lax
from jax.experimental import pallas as pl
from jax.experimental.pallas import tpu as pltpu

F32 = jnp.float32
BF16 = jnp.bfloat16

D_MODEL = 1024
D_CONV = 512
CONV_W = 3
HEADS = 8
HEAD_DIM = 64
D_RWKV = HEADS * HEAD_DIM
PAIR = 2 * HEAD_DIM
N_PAIRS = HEADS // 2
GN_EPS = 64e-5
RMS_EPS = 1e-6
OFF_RKV = 3 * D_CONV
OFF_GATE = OFF_RKV + 3 * D_RWKV
D_IN = OFF_GATE + 2 * D_MODEL

PEER_HEADS = 8
PEER_KEYS = 128
PEER_HALF = 128
PEER_TOPK = 16
PEER_SEL = PEER_HEADS * PEER_TOPK
TOK_TILE = 128
PEER_SLOTS = 4

VMEM_LIMIT_BYTES = 56 * 1024 * 1024


def _dot(a, b):
    return jnp.dot(a.astype(BF16), b.astype(BF16), preferred_element_type=F32)


def _dot_nt(a, b):
    return lax.dot_general(a.astype(BF16), b.astype(BF16), (((1,), (1,)), ((), ())),
                           preferred_element_type=F32)


def _dot_tn(a, b):
    return lax.dot_general(a.astype(BF16), b.astype(BF16), (((0,), (0,)), ((), ())),
                           preferred_element_type=F32)


def _split(a):
    hi = a.astype(BF16)
    lo = (a - hi.astype(F32)).astype(BF16)
    return hi, lo


def _dot_hl(a, w_bf16):
    hi, lo = _split(a)
    return (jnp.dot(hi, w_bf16, preferred_element_type=F32)
            + jnp.dot(lo, w_bf16, preferred_element_type=F32))


def _sigmoid(x):
    return 1.0 / (1.0 + jnp.exp(-x))


def _rms_norm(x, g):
    return x * lax.rsqrt(jnp.mean(x * x, axis=-1, keepdims=True) + RMS_EPS) * g


def _shift_rows(a, carry, n):
    rolled = pltpu.roll(a, n, 0)
    row = lax.broadcasted_iota(jnp.int32, a.shape, 0)
    for i in range(n):
        rolled = jnp.where(row == i, carry[i:i + 1], rolled)
    return rolled


def _pre_kernel(x_ref, conv0_ref, shift0_ref, n1g_ref, w_in_ref, convw_ref, mu_rkv_ref, mu_wag_ref,
                w0_ref, w1_ref, w2_ref, a0_ref, a1_ref, a2_ref, g1_ref, g2_ref, kk_ref, ka_ref, rk_ref,
                w_pa_ref, hsum_ref,
                r_out, lw_out, k_out, v_out, a_out, b_out, bonus_out, g_out, ma_out, sgb_out,
                nshift_out, nconv_out,
                xn_c, zrkv_c, u_c):
    t = pl.program_id(1)
    tt = x_ref.shape[1]
    xn = _rms_norm(x_ref[0], n1g_ref[...])
    xnb = xn.astype(BF16)

    @pl.when(t == 0)
    def _():
        prev = jnp.broadcast_to(shift0_ref[0], (8, D_MODEL))
        xn_c[...] = prev
        zrkv_c[...] = jnp.dot(prev.astype(BF16), w_in_ref[:, OFF_RKV:OFF_GATE],
                              preferred_element_type=F32)
        u_c[0:2, :] = conv0_ref[0]

    zbch = jnp.dot(xnb, w_in_ref[:, 0:OFF_RKV], preferred_element_type=F32)
    zb = zbch[:, 0:D_CONV]
    u = zbch[:, D_CONV:2 * D_CONV] * zbch[:, 2 * D_CONV:3 * D_CONV]
    u_prev = u_c[0:2, :]
    u1 = _shift_rows(u, u_prev[1:2], 1)
    u2 = _shift_rows(u, u_prev, 2)
    cw = convw_ref[...]
    y_a = zb * (cw[0:1] * u2 + cw[1:2] * u1 + cw[2:3] * u)
    u_last = u[tt - 2:tt, :]
    nconv_out[0] = u_last
    u_c[0:2, :] = u_last

    zg = jnp.dot(xnb, w_in_ref[:, OFF_GATE:D_IN], preferred_element_type=F32)
    ma_out[0] = _sigmoid(zg[:, 0:D_MODEL]) * _dot(y_a, w_pa_ref[...])
    sgb_out[0] = _sigmoid(zg[:, D_MODEL:2 * D_MODEL])

    zrkv = jnp.dot(xnb, w_in_ref[:, OFF_RKV:OFF_GATE], preferred_element_type=F32)
    zprev = _shift_rows(zrkv, zrkv_c[0:1, :], 1)
    zs = zrkv + mu_rkv_ref[...] * (zprev - zrkv)
    xprev = _shift_rows(xn, xn_c[0:1, :], 1)
    dx = xprev - xn
    mu = mu_wag_ref[...]
    xw = xn + dx * mu[0:1]
    xa = xn + dx * mu[1:2]
    xg = xn + dx * mu[2:3]
    xn_last = xn[tt - 1:tt, :]
    nshift_out[0] = xn_last
    xn_c[0:1, :] = xn_last
    zrkv_c[0:1, :] = zrkv[tt - 1:tt, :]

    wl = w0_ref[...] + _dot(jnp.tanh(_dot(xw, w1_ref[...])), w2_ref[...])
    softplus = jnp.maximum(-wl, 0.0) + jnp.log(1.0 + jnp.exp(-jnp.abs(wl)))
    lw_out[0] = -jnp.exp(-softplus - 0.5)
    a_sig = _sigmoid(a0_ref[...] + _dot(_dot(xa, a1_ref[...]), a2_ref[...]))
    g_out[0] = _dot(_sigmoid(_dot(xg, g1_ref[...])), g2_ref[...])

    r = zs[:, 0:D_RWKV]
    k = zs[:, D_RWKV:2 * D_RWKV]
    v = zs[:, 2 * D_RWKV:3 * D_RWKV]
    hsum = hsum_ref[...]
    kk = k * kk_ref[...]
    kk = kk / jnp.maximum(jnp.sqrt(_dot_hl(kk * kk, hsum)), 1e-12)
    k = k * (1.0 + (a_sig - 1.0) * ka_ref[...])
    r_out[0] = r
    k_out[0] = k
    v_out[0] = v
    a_out[0] = -kk
    b_out[0] = kk * a_sig
    bonus_out[0] = _dot_hl(r * k * rk_ref[...], hsum) * v


def _pre_call(x, conv0, shift0, w, tt):
    bsz, seq, _ = x.shape
    grid = (bsz, seq // tt)
    row = lambda b, t: (b, t, 0)
    per_b = lambda b, t: (b, 0, 0)
    const2 = lambda b, t: (0, 0)

    def tok(c):
        return pl.BlockSpec((1, tt, c), row)

    def full(a):
        return pl.BlockSpec(a.shape, const2)

    weights = (w["norm1_g"], w["w_in"], w["conv_w"], w["mu_rkv"], w["mu_wag"], w["w0"], w["w1"], w["w2"],
               w["a0"], w["a1"], w["a2"], w["g1"], w["g2"], w["k_k"], w["k_a"], w["r_k"], w["w_pa"],
               w["hsum"])
    in_specs = [tok(D_MODEL), pl.BlockSpec((1, CONV_W - 1, D_CONV), per_b),
                pl.BlockSpec((1, 1, D_MODEL), per_b)] + [full(a) for a in weights]
    tok_shape = lambda c: jax.ShapeDtypeStruct((bsz, seq, c), F32)
    out_shape = [tok_shape(D_RWKV)] * 8 + [tok_shape(D_MODEL)] * 2 + [
        jax.ShapeDtypeStruct((bsz, 1, D_MODEL), F32),
        jax.ShapeDtypeStruct((bsz, CONV_W - 1, D_CONV), F32)]
    out_specs = [tok(D_RWKV)] * 8 + [tok(D_MODEL)] * 2 + [
        pl.BlockSpec((1, 1, D_MODEL), per_b), pl.BlockSpec((1, CONV_W - 1, D_CONV), per_b)]
    return pl.pallas_call(
        _pre_kernel, grid=grid, in_specs=in_specs, out_specs=out_specs, out_shape=out_shape,
        scratch_shapes=[pltpu.VMEM((8, D_MODEL), F32), pltpu.VMEM((8, 3 * D_RWKV), F32),
                        pltpu.VMEM((8, D_CONV), F32)],
        compiler_params=pltpu.CompilerParams(dimension_semantics=("arbitrary", "arbitrary"),
                                             vmem_limit_bytes=VMEM_LIMIT_BYTES),
        name="pre",
    )(x, conv0, shift0, *weights)


def _wkv_kernel(r_ref, lw_ref, k_ref, v_ref, a_ref, b_ref, s0_ref, tri_ref,
                y_out, s_out, s_c):
    c = pl.program_id(2)
    L = r_ref.shape[1]
    lane = lax.broadcasted_iota(jnp.int32, (L, PAIR), 1)
    first = lane < HEAD_DIM
    s_row = lax.broadcasted_iota(jnp.int32, (PAIR, PAIR), 0)
    s_col = lax.broadcasted_iota(jnp.int32, (PAIR, PAIR), 1)
    s_mask = (s_row < HEAD_DIM) == (s_col < HEAD_DIM)

    @pl.when(c == 0)
    def _():
        s0 = s0_ref[0]
        z = jnp.zeros((HEAD_DIM, HEAD_DIM), F32)
        s_c[...] = jnp.concatenate([jnp.concatenate([s0[0], z], axis=1),
                                    jnp.concatenate([z, s0[1]], axis=1)], axis=0)

    S = s_c[...]
    r = r_ref[0]
    lw = lw_ref[0]
    k = k_ref[0]
    v = v_ref[0]
    a = a_ref[0]
    b = b_ref[0]

    l1 = lw.astype(BF16)
    r1 = lw - l1.astype(F32)
    l2 = r1.astype(BF16)
    l3 = (r1 - l2.astype(F32)).astype(BF16)
    tri = tri_ref[...]
    cum = (jnp.dot(tri, l1, preferred_element_type=F32) + jnp.dot(tri, l2, preferred_element_type=F32)
           + jnp.dot(tri, l3, preferred_element_type=F32))
    cum_l = cum[L - 1:L, :]
    w_inc = jnp.exp(cum)
    w_inv = jnp.exp(-cum)
    at = a * jnp.exp(cum - lw)
    bt = b * w_inv
    kt = k * w_inv
    rt = r * w_inc
    dec = jnp.exp(cum_l - cum)

    row = lax.broadcasted_iota(jnp.int32, (L, L), 0)
    col = lax.broadcasted_iota(jnp.int32, (L, L), 1)
    strict = row > col
    incl = row >= col

    def per_head(x):
        return jnp.where(first, x, 0.0), jnp.where(first, 0.0, x)

    def merge(x1, x2):
        return jnp.where(first, x1, x2)

    at1, at2 = per_head(at)
    rt1, rt2 = per_head(rt)
    zero = jnp.zeros((L, L), F32)
    mab = [jnp.where(strict, _dot_nt(x, bt), zero) for x in (at1, at2)]
    mak = [jnp.where(strict, _dot_nt(x, kt), zero) for x in (at1, at2)]
    nrb = [jnp.where(incl, _dot_nt(x, bt), zero) for x in (rt1, rt2)]
    nrk = [jnp.where(incl, _dot_nt(x, kt), zero) for x in (rt1, rt2)]

    U = _dot_nt(at, S) + merge(_dot(mak[0], v), _dot(mak[1], v))
    n = 1
    while n < L:
        U = U + merge(_dot(mab[0], U), _dot(mab[1], U))
        n *= 2
        if n < L:
            mab = [_dot(m, m) for m in mab]
    y = (_dot_nt(rt, S) + merge(_dot(nrb[0], U), _dot(nrb[1], U))
         + merge(_dot(nrk[0], v), _dot(nrk[1], v)))
    y_out[0] = y

    s_new = S * jnp.exp(cum_l) + jnp.where(s_mask, _dot_tn(U, b * dec) + _dot_tn(v, k * dec), 0.0)
    s_c[...] = s_new
    s_out[0, 0] = s_new[0:HEAD_DIM, 0:HEAD_DIM]
    s_out[0, 1] = s_new[HEAD_DIM:PAIR, HEAD_DIM:PAIR]


def _wkv_call(r, lw, k, v, a, b, s0, chunk):
    bsz, seq, _ = r.shape
    grid = (bsz, N_PAIRS, seq // chunk)
    tok = pl.BlockSpec((1, chunk, PAIR), lambda bi, p, c: (bi, c, p))
    st = pl.BlockSpec((1, 2, HEAD_DIM, HEAD_DIM), lambda bi, p, c: (bi, p, 0, 0))
    tri = (jnp.arange(chunk)[:, None] >= jnp.arange(chunk)[None, :]).astype(BF16)
    return pl.pallas_call(
        _wkv_kernel, grid=grid,
        in_specs=[tok] * 6 + [st, pl.BlockSpec((chunk, chunk), lambda bi, p, c: (0, 0))],
        out_specs=[tok, st],
        out_shape=[jax.ShapeDtypeStruct((bsz, seq, D_RWKV), F32),
                   jax.ShapeDtypeStruct((bsz, HEADS, HEAD_DIM, HEAD_DIM), F32)],
        scratch_shapes=[pltpu.VMEM((PAIR, PAIR), F32)],
        compiler_params=pltpu.CompilerParams(
            dimension_semantics=("arbitrary", "arbitrary", "arbitrary")),
        name="wkv",
    )(r, lw, k, v, a, b, s0, tri)


def _post_kernel(y_ref, bonus_ref, g_ref, ma_ref, sgb_ref, x_ref, gnw_ref, gnb_ref, hsum_ref,
                 w_pb_ref, w_o_ref, n2g_ref, wq_hi_ref, wq_lo_ref,
                 x2_out, xn2_out, q_out):
    y = y_ref[...]
    hsum = hsum_ref[...]
    mean = _dot_hl(y, hsum) * (1.0 / HEAD_DIM)
    d = y - mean
    var = _dot_hl(d * d, hsum) * (1.0 / HEAD_DIM)
    yn = d * lax.rsqrt(var + GN_EPS) * gnw_ref[...] + gnb_ref[...] + bonus_ref[...]
    y_b = yn * g_ref[...]
    merged = ma_ref[...] + sgb_ref[...] * _dot(y_b, w_pb_ref[...])
    x2 = x_ref[...] + _dot(merged, w_o_ref[...])
    x2_out[...] = x2
    xn2 = _rms_norm(x2, n2g_ref[...])
    xn2_out[...] = xn2
    hi, lo = _split(xn2)
    wq_hi = wq_hi_ref[...]
    q_out[...] = (jnp.dot(hi, wq_hi, preferred_element_type=F32)
                  + jnp.dot(lo, wq_hi, preferred_element_type=F32)
                  + jnp.dot(hi, wq_lo_ref[...], preferred_element_type=F32))


def _post_call(y, bonus, g, ma, sgb, x, w, tt):
    n = y.shape[0]
    row = lambda i: (i, 0)
    const = lambda i: (0, 0)
    tok = lambda c: pl.BlockSpec((tt, c), row)
    weights = (w["gn_w"], w["gn_b"], w["hsum"], w["w_pb"], w["w_o"], w["norm2_g"], w["wq_hi"], w["wq_lo"])
    d_q = w["wq_hi"].shape[1]
    return pl.pallas_call(
        _post_kernel, grid=(n // tt,),
        in_specs=[tok(D_RWKV)] * 3 + [tok(D_MODEL)] * 3 + [pl.BlockSpec(a.shape, const) for a in weights],
        out_specs=[tok(D_MODEL), tok(D_MODEL), tok(d_q)],
        out_shape=[jax.ShapeDtypeStruct((n, D_MODEL), F32)] * 2 + [jax.ShapeDtypeStruct((n, d_q), F32)],
        compiler_params=pltpu.CompilerParams(dimension_semantics=("arbitrary",),
                                             vmem_limit_bytes=VMEM_LIMIT_BYTES),
        name="post",
    )(y, bonus, g, ma, sgb, x, *weights)


TOPK_HEADS = 4
STAIR_COUNTS = tuple(PEER_TOPK // (a + 1) for a in range(8))
STAIR_ROWS = 16 + 8 * 7 + 8


def _extract_max(s, iota, n_rows):
    m = jnp.max(s, axis=0, keepdims=True)
    idx = jnp.min(jnp.where(s == m, iota, n_rows), axis=0, keepdims=True)
    return m, idx, iota == idx


def _topk_kernel(q_ref, khi_ref, klo_ref, e_out, g_out, s_scr, v_scr, i_scr, c_scr, ci_scr, sc_scr):
    nt = lambda x, y: lax.dot_general(x, y, (((1,), (1,)), ((), ())), preferred_element_type=F32)
    for c in range(2 * TOPK_HEADS):
        h, p = divmod(c, 2)
        q_hi, q_lo = _split(q_ref[:, c * PEER_HALF:(c + 1) * PEER_HALF])
        k_hi = khi_ref[h, p]
        s_scr[c] = nt(k_hi, q_hi) + nt(k_hi, q_lo) + nt(klo_ref[h, p], q_hi)

    iota = lax.broadcasted_iota(jnp.int32, (PEER_KEYS, TOK_TILE), 0)

    def sub_key_step(j, carry):
        for c in range(2 * TOPK_HEADS):
            s = s_scr[c]
            m, idx, hit = _extract_max(s, iota, PEER_KEYS)
            v_scr[c, pl.ds(j, 1), :] = m
            i_scr[c, pl.ds(j, 1), :] = idx
            s_scr[c] = jnp.where(hit, -jnp.inf, s)
        return carry

    lax.fori_loop(0, PEER_TOPK, sub_key_step, 0)

    row8 = lax.broadcasted_iota(jnp.int32, (8, TOK_TILE), 0)
    for h in range(TOPK_HEADS):
        v1, i1 = v_scr[2 * h], i_scr[2 * h] * PEER_KEYS
        v2, i2 = v_scr[2 * h + 1], i_scr[2 * h + 1]
        vals = [v1[0:1] + v2]
        idxs = [i1[0:1] + i2]
        for a in range(1, 8):
            vals.append(jnp.where(row8 < STAIR_COUNTS[a], v1[a:a + 1] + v2[0:8], -jnp.inf))
            idxs.append(i1[a:a + 1] + i2[0:8])
        vals.append(v1[8:16] + v2[0:1])
        idxs.append(i1[8:16] + i2[0:1])
        c_scr[h] = jnp.concatenate(vals, axis=0)
        ci_scr[h] = jnp.concatenate(idxs, axis=0)

    iota_c = lax.broadcasted_iota(jnp.int32, (STAIR_ROWS, TOK_TILE), 0)

    def expert_step(j, carry):
        for h in range(TOPK_HEADS):
            s = c_scr[h]
            m, _, hit = _extract_max(s, iota_c, STAIR_ROWS)
            sc_scr[h, pl.ds(j, 1), :] = m
            e_out[pl.ds(h * PEER_TOPK + j, 1), :] = jnp.max(jnp.where(hit, ci_scr[h], -1), axis=0,
                                                             keepdims=True)
            c_scr[h] = jnp.where(hit, -jnp.inf, s)
        return carry

    lax.fori_loop(0, PEER_TOPK, expert_step, 0)

    for h in range(TOPK_HEADS):
        sc = sc_scr[h]
        e = jnp.exp(sc - sc[0:1])
        g_out[h * PEER_TOPK:(h + 1) * PEER_TOPK, :] = e / jnp.sum(e, axis=0, keepdims=True)


def _topk_call(q, khi, klo):
    n = q.shape[0]
    rows = TOPK_HEADS * PEER_TOPK
    sel = pl.BlockSpec((rows, TOK_TILE), lambda i, h: (h, i))
    keys = pl.BlockSpec((TOPK_HEADS, 2, PEER_KEYS, PEER_HALF), lambda i, h: (h, 0, 0, 0))
    chains = 2 * TOPK_HEADS
    return pl.pallas_call(
        _topk_kernel, grid=(n // TOK_TILE, PEER_HEADS // TOPK_HEADS),
        in_specs=[pl.BlockSpec((TOK_TILE, chains * PEER_HALF), lambda i, h: (i, h)), keys, keys],
        out_specs=[sel, sel],
        out_shape=[jax.ShapeDtypeStruct((PEER_SEL, n), jnp.int32), jax.ShapeDtypeStruct((PEER_SEL, n), F32)],
        scratch_shapes=[pltpu.VMEM((chains, PEER_KEYS, TOK_TILE), F32),
                        pltpu.VMEM((chains, PEER_TOPK, TOK_TILE), F32),
                        pltpu.VMEM((chains, PEER_TOPK, TOK_TILE), jnp.int32),
                        pltpu.VMEM((TOPK_HEADS, STAIR_ROWS, TOK_TILE), F32),
                        pltpu.VMEM((TOPK_HEADS, STAIR_ROWS, TOK_TILE), jnp.int32),
                        pltpu.VMEM((TOPK_HEADS, PEER_TOPK, TOK_TILE), F32)],
        compiler_params=pltpu.CompilerParams(dimension_semantics=("arbitrary", "arbitrary")),
        name="topk",
    )(q, khi, klo)


def _pack_kernel(u_ref, v_ref, w_out):
    ub = lax.bitcast_convert_type(u_ref[...].astype(BF16).astype(F32), jnp.uint32)
    vb = lax.bitcast_convert_type(v_ref[...].astype(BF16).astype(F32), jnp.uint32)
    w_out[...] = (ub & jnp.uint32(0xFFFF0000)) | (vb >> 16)


def _pack_call(u, v):
    n, d = u.shape
    rows = 512
    blk = pl.BlockSpec((rows, d), lambda i: (i, 0))
    return pl.pallas_call(
        _pack_kernel, grid=(n // rows,), in_specs=[blk, blk], out_specs=blk,
        out_shape=jax.ShapeDtypeStruct((n, d), jnp.uint32),
        compiler_params=pltpu.CompilerParams(dimension_semantics=("arbitrary",)),
        name="pack",
    )(u, v)


def _peer_kernel(e_ref, g_ref, xn2_ref, x2_ref, nfg_ref, w_hbm, y_out,
                 e_smem, g_vmem, wbuf, sem_e, sem_w):
    to_smem = pltpu.make_async_copy(e_ref, e_smem, sem_e)
    to_smem.start()
    g_vmem[...] = g_ref[...].T
    to_smem.wait()

    def row_copy(t, slot, j):
        return pltpu.make_async_copy(w_hbm.at[e_smem[j, t]], wbuf.at[slot, :, j, :], sem_w.at[slot])

    def wait_slot(slot):
        pltpu.make_async_copy(wbuf.at[(slot + 1) % PEER_SLOTS], wbuf.at[slot], sem_w.at[slot]).wait()

    def mix_pair(t, slots, t_next, next_slots):
        starts = [(k, j) for k in range(2) for j in range(PEER_SEL)]
        per_step = len(starts) // 32

        def start_some(step):
            for k, j in starts[step * per_step:(step + 1) * per_step]:
                row_copy(t_next + k, next_slots[k], j).start(priority=j % 2)

        xs = [jnp.broadcast_to(xn2_ref[pl.ds(t + k, 1), :], (8, D_MODEL)).astype(BF16) for k in range(2)]
        acts = [jnp.zeros((8, PEER_SEL), F32) for _ in range(2)]
        step = 0
        for s in range(8):
            for k in range(2):
                w = wbuf[slots[k], s]
                u = lax.bitcast_convert_type(w & jnp.uint32(0xFFFF0000), F32).astype(BF16)
                acts[k] = acts[k] + lax.dot_general(xs[k][:, s * 128:(s + 1) * 128], u,
                                                    (((1,), (1,)), ((), ())), preferred_element_type=F32)
                start_some(step)
                step += 1
        coefs = []
        for k in range(2):
            act = acts[k]
            gelu = 0.5 * act * (1.0 + jnp.tanh(0.7978845608028654 * (act + 0.044715 * (act * act * act))))
            coefs.append((g_vmem[pl.ds(t + k, 1), :] * gelu).astype(BF16))
        outs = [[], []]
        x2 = [x2_ref[pl.ds(t + k, 1), :] for k in range(2)]
        for s in range(8):
            for k in range(2):
                w = wbuf[slots[k], s]
                v = lax.bitcast_convert_type(w << 16, F32).astype(BF16)
                mix = jnp.dot(coefs[k], v, preferred_element_type=F32)
                outs[k].append(x2[k][:, s * 128:(s + 1) * 128] + mix[0:1])
                start_some(step)
                step += 1
        for k in range(2):
            y_out[pl.ds(t + k, 1), :] = _rms_norm(jnp.concatenate(outs[k], axis=1), nfg_ref[...])

    for j in range(PEER_SEL):
        row_copy(0, 0, j).start(priority=j % 2)
        row_copy(1, 1, j).start(priority=j % 2)

    def body(i, carry):
        t = PEER_SLOTS * i
        wait_slot(0)
        wait_slot(1)
        mix_pair(t, (0, 1), t + 2, (2, 3))
        wait_slot(2)
        wait_slot(3)
        mix_pair(t + 2, (2, 3), jnp.minimum(t + 4, TOK_TILE - 2), (0, 1))
        return carry

    lax.fori_loop(0, TOK_TILE // PEER_SLOTS, body, 0)
    wait_slot(0)
    wait_slot(1)


def _peer_call(eidx, gate, xn2, x2, nfg, w3):
    n = xn2.shape[0]
    sel = pl.BlockSpec((PEER_SEL, TOK_TILE), lambda i: (0, i))
    tok = pl.BlockSpec((TOK_TILE, D_MODEL), lambda i: (i, 0))
    return pl.pallas_call(
        _peer_kernel, grid=(n // TOK_TILE,),
        in_specs=[sel, sel, tok, tok, pl.BlockSpec((1, D_MODEL), lambda i: (0, 0)),
                  pl.BlockSpec(memory_space=pl.ANY)],
        out_specs=tok,
        out_shape=jax.ShapeDtypeStruct((n, D_MODEL), F32),
        scratch_shapes=[pltpu.SMEM((PEER_SEL, TOK_TILE), jnp.int32),
                        pltpu.VMEM((TOK_TILE, PEER_SEL), F32),
                        pltpu.VMEM((PEER_SLOTS, 8, PEER_SEL, 128), jnp.uint32),
                        pltpu.SemaphoreType.DMA,
                        pltpu.SemaphoreType.DMA((PEER_SLOTS,))],
        compiler_params=pltpu.CompilerParams(dimension_semantics=("arbitrary",)),
        name="peer",
    )(eidx, gate, xn2, x2, nfg, w3)


def _tile_choices(bsz, seq):
    tt = min(seq, 256)
    chunk = min(seq, 64)
    post = min(bsz * seq, 256)
    return tt, chunk, post


def _run_trunk(x, st_conv, st_shift, st_wkv, w):
    bsz, seq, _ = x.shape
    n = bsz * seq
    tt, chunk, post_tt = _tile_choices(bsz, seq)
    (r, lw, k, v, a, b, bonus, g, ma, sgb, new_shift, new_conv) = _pre_call(
        x, st_conv, st_shift.reshape(bsz, 1, D_MODEL), w, tt)
    y, new_wkv = _wkv_call(r, lw, k, v, a, b, st_wkv, chunk)
    flat = lambda t: t.reshape(n, t.shape[-1])
    x2, xn2, q = _post_call(flat(y), flat(bonus), flat(g), flat(ma), flat(sgb), flat(x), w, post_tt)
    eidx, gate = _topk_call(q, w["keys_hi"], w["keys_lo"])
    out = _peer_call(eidx, gate, xn2, x2, w["norm_f_g"], w["peer_w"])
    return (out.reshape(bsz, seq, D_MODEL), new_conv[None], new_shift.reshape(1, bsz, D_MODEL),
            new_wkv[None])


def kernel(x_prompt, x_sample, state_conv, state_shift, state_wkv, norm1_g, w_in, conv_w, mu_rkv, mu_wag,
           w0, w1, w2, a0, a1, a2, g1, g2, k_k, k_a, r_k, gn_w, gn_b, w_pa, w_pb, w_o, norm2_g,
           peer_wq, peer_keys, peer_u, peer_v, norm_f_g):
    row = lambda t: t.reshape(1, -1)
    head = jnp.arange(D_RWKV) // HEAD_DIM
    wq_hi, wq_lo = _split(peer_wq[0])
    keys = peer_keys[0]
    keys_hi, keys_lo = _split(keys)
    w = dict(
        norm1_g=norm1_g, w_in=w_in[0].astype(BF16), conv_w=conv_w[0], mu_rkv=mu_rkv, mu_wag=mu_wag[0],
        w0=w0, w1=w1[0].astype(BF16), w2=w2[0].astype(BF16), a0=a0, a1=a1[0].astype(BF16),
        a2=a2[0].astype(BF16), g1=g1[0].astype(BF16), g2=g2[0].astype(BF16), k_k=k_k, k_a=k_a,
        r_k=row(r_k[0]), gn_w=gn_w, gn_b=gn_b, w_pa=w_pa[0].astype(BF16), w_pb=w_pb[0].astype(BF16),
        w_o=w_o[0].astype(BF16), norm2_g=norm2_g, wq_hi=wq_hi, wq_lo=wq_lo, keys_hi=keys_hi,
        keys_lo=keys_lo, norm_f_g=row(norm_f_g),
        peer_w=_pack_call(peer_u[0], peer_v[0]).reshape(-1, 8, 128),
        hsum=(head[:, None] == head[None, :]).astype(BF16),
    )
    bp = x_prompt.shape[0]
    zero_conv = jnp.zeros((bp, CONV_W - 1, D_CONV), F32)
    zero_shift = jnp.zeros((bp, D_MODEL), F32)
    zero_wkv = jnp.zeros((bp, HEADS, HEAD_DIM, HEAD_DIM), F32)
    y_s, conv_s, shift_s, wkv_s = _run_trunk(x_sample, state_conv[0], state_shift[0], state_wkv[0], w)
    y_p, conv_p, shift_p, wkv_p = _run_trunk(x_prompt, zero_conv, zero_shift, zero_wkv, w)
    return (y_p, y_s, conv_p, shift_p, wkv_p, conv_s, shift_s, wkv_s)
```

```python
import functools

import jax
import jax.numpy as jnp
from jax import lax
from jax.experimental import pallas as pl
from jax.experimental.pallas import tpu as pltpu
from jax.experimental.pallas import tpu_sc as plsc

F32 = jnp.float32
BF16 = jnp.bfloat16

D_MODEL = 1024
D_CONV = 512
CONV_W = 3
HEADS = 8
HEAD_DIM = 64
D_RWKV = HEADS * HEAD_DIM
PAIR = 2 * HEAD_DIM
N_PAIRS = HEADS // 2
GN_EPS = 64e-5
RMS_EPS = 1e-6
OFF_RKV = 3 * D_CONV
OFF_GATE = OFF_RKV + 3 * D_RWKV
D_IN = OFF_GATE + 2 * D_MODEL

PEER_HEADS = 8
PEER_KEYS = 128
PEER_HALF = 128
PEER_TOPK = 16
PEER_SEL = PEER_HEADS * PEER_TOPK
TOK_TILE = 128
PEER_SLOTS = 4
SC_CORES = 2
SC_SUBCORES = 16
SC_WINDOW = 64
SC_SHARE_PERCENT = 70
SC_MIN_TILES = 32
STAGE_TOKENS = 8

VMEM_LIMIT_BYTES = 56 * 1024 * 1024


def _dot(a, b):
    return jnp.dot(a.astype(BF16), b.astype(BF16), preferred_element_type=F32)


def _dot_nt(a, b):
    return lax.dot_general(a.astype(BF16), b.astype(BF16), (((1,), (1,)), ((), ())),
                           preferred_element_type=F32)


def _dot_tn(a, b):
    return lax.dot_general(a.astype(BF16), b.astype(BF16), (((0,), (0,)), ((), ())),
                           preferred_element_type=F32)


def _split(a):
    hi = a.astype(BF16)
    lo = (a - hi.astype(F32)).astype(BF16)
    return hi, lo


def _dot_hl(a, w_bf16):
    hi, lo = _split(a)
    return (jnp.dot(hi, w_bf16, preferred_element_type=F32)
            + jnp.dot(lo, w_bf16, preferred_element_type=F32))


def _sigmoid(x):
    return 1.0 / (1.0 + jnp.exp(-x))


def _rms_norm(x, g):
    return x * lax.rsqrt(jnp.mean(x * x, axis=-1, keepdims=True) + RMS_EPS) * g


def _shift_rows(a, carry, n):
    rolled = pltpu.roll(a, n, 0)
    row = lax.broadcasted_iota(jnp.int32, a.shape, 0)
    for i in range(n):
        rolled = jnp.where(row == i, carry[i:i + 1], rolled)
    return rolled


def _pre_kernel(x_ref, conv0_ref, shift0_ref, n1g_ref, w_in_ref, convw_ref, mu_rkv_ref, mu_wag_ref,
                w0_ref, w1_ref, w2_ref, a0_ref, a1_ref, a2_ref, g1_ref, g2_ref, kk_ref, ka_ref, rk_ref,
                w_pa_ref, hsum_ref,
                r_out, lw_out, k_out, v_out, a_out, b_out, bonus_out, g_out, ma_out, sgb_out,
                nshift_out, nconv_out,
                xn_c, zrkv_c, u_c):
    t = pl.program_id(1)
    tt = x_ref.shape[1]
    xn = _rms_norm(x_ref[0], n1g_ref[...])
    xnb = xn.astype(BF16)

    @pl.when(t == 0)
    def _():
        prev = jnp.broadcast_to(shift0_ref[0], (8, D_MODEL))
        xn_c[...] = prev
        zrkv_c[...] = jnp.dot(prev.astype(BF16), w_in_ref[:, OFF_RKV:OFF_GATE],
                              preferred_element_type=F32)
        u_c[0:2, :] = conv0_ref[0]

    zbch = jnp.dot(xnb, w_in_ref[:, 0:OFF_RKV], preferred_element_type=F32)
    zb = zbch[:, 0:D_CONV]
    u = zbch[:, D_CONV:2 * D_CONV] * zbch[:, 2 * D_CONV:3 * D_CONV]
    u_prev = u_c[0:2, :]
    u1 = _shift_rows(u, u_prev[1:2], 1)
    u2 = _shift_rows(u, u_prev, 2)
    cw = convw_ref[...]
    y_a = zb * (cw[0:1] * u2 + cw[1:2] * u1 + cw[2:3] * u)
    u_last = u[tt - 2:tt, :]
    nconv_out[0] = u_last
    u_c[0:2, :] = u_last

    zg = jnp.dot(xnb, w_in_ref[:, OFF_GATE:D_IN], preferred_element_type=F32)
    ma_out[0] = _sigmoid(zg[:, 0:D_MODEL]) * _dot(y_a, w_pa_ref[...])
    sgb_out[0] = _sigmoid(zg[:, D_MODEL:2 * D_MODEL])

    zrkv = jnp.dot(xnb, w_in_ref[:, OFF_RKV:OFF_GATE], preferred_element_type=F32)
    zprev = _shift_rows(zrkv, zrkv_c[0:1, :], 1)
    zs = zrkv + mu_rkv_ref[...] * (zprev - zrkv)
    xprev = _shift_rows(xn, xn_c[0:1, :], 1)
    dx = xprev - xn
    mu = mu_wag_ref[...]
    xw = xn + dx * mu[0:1]
    xa = xn + dx * mu[1:2]
    xg = xn + dx * mu[2:3]
    xn_last = xn[tt - 1:tt, :]
    nshift_out[0] = xn_last
    xn_c[0:1, :] = xn_last
    zrkv_c[0:1, :] = zrkv[tt - 1:tt, :]

    wl = w0_ref[...] + _dot(jnp.tanh(_dot(xw, w1_ref[...])), w2_ref[...])
    softplus = jnp.maximum(-wl, 0.0) + jnp.log(1.0 + jnp.exp(-jnp.abs(wl)))
    lw_out[0] = -jnp.exp(-softplus - 0.5)
    a_sig = _sigmoid(a0_ref[...] + _dot(_dot(xa, a1_ref[...]), a2_ref[...]))
    g_out[0] = _dot(_sigmoid(_dot(xg, g1_ref[...])), g2_ref[...])

    r = zs[:, 0:D_RWKV]
    k = zs[:, D_RWKV:2 * D_RWKV]
    v = zs[:, 2 * D_RWKV:3 * D_RWKV]
    hsum = hsum_ref[...]
    kk = k * kk_ref[...]
    kk = kk / jnp.maximum(jnp.sqrt(_dot_hl(kk * kk, hsum)), 1e-12)
    k = k * (1.0 + (a_sig - 1.0) * ka_ref[...])
    r_out[0] = r
    k_out[0] = k
    v_out[0] = v
    a_out[0] = -kk
    b_out[0] = kk * a_sig
    bonus_out[0] = _dot_hl(r * k * rk_ref[...], hsum) * v


def _pre_call(x, conv0, shift0, w, tt):
    bsz, seq, _ = x.shape
    grid = (bsz, seq // tt)
    row = lambda b, t: (b, t, 0)
    per_b = lambda b, t: (b, 0, 0)
    const2 = lambda b, t: (0, 0)

    def tok(c):
        return pl.BlockSpec((1, tt, c), row)

    def full(a):
        return pl.BlockSpec(a.shape, const2)

    weights = (w["norm1_g"], w["w_in"], w["conv_w"], w["mu_rkv"], w["mu_wag"], w["w0"], w["w1"], w["w2"],
               w["a0"], w["a1"], w["a2"], w["g1"], w["g2"], w["k_k"], w["k_a"], w["r_k"], w["w_pa"],
               w["hsum"])
    in_specs = [tok(D_MODEL), pl.BlockSpec((1, CONV_W - 1, D_CONV), per_b),
                pl.BlockSpec((1, 1, D_MODEL), per_b)] + [full(a) for a in weights]
    tok_shape = lambda c: jax.ShapeDtypeStruct((bsz, seq, c), F32)
    out_shape = [tok_shape(D_RWKV)] * 8 + [tok_shape(D_MODEL)] * 2 + [
        jax.ShapeDtypeStruct((bsz, 1, D_MODEL), F32),
        jax.ShapeDtypeStruct((bsz, CONV_W - 1, D_CONV), F32)]
    out_specs = [tok(D_RWKV)] * 8 + [tok(D_MODEL)] * 2 + [
        pl.BlockSpec((1, 1, D_MODEL), per_b), pl.BlockSpec((1, CONV_W - 1, D_CONV), per_b)]
    return pl.pallas_call(
        _pre_kernel, grid=grid, in_specs=in_specs, out_specs=out_specs, out_shape=out_shape,
        scratch_shapes=[pltpu.VMEM((8, D_MODEL), F32), pltpu.VMEM((8, 3 * D_RWKV), F32),
                        pltpu.VMEM((8, D_CONV), F32)],
        compiler_params=pltpu.CompilerParams(dimension_semantics=("arbitrary", "arbitrary"),
                                             vmem_limit_bytes=VMEM_LIMIT_BYTES),
        name="pre",
    )(x, conv0, shift0, *weights)


def _wkv_kernel(r_ref, lw_ref, k_ref, v_ref, a_ref, b_ref, s0_ref, tri_ref,
                y_out, s_out, s_c):
    c = pl.program_id(2)
    L = r_ref.shape[1]
    lane = lax.broadcasted_iota(jnp.int32, (L, PAIR), 1)
    first = lane < HEAD_DIM
    s_row = lax.broadcasted_iota(jnp.int32, (PAIR, PAIR), 0)
    s_col = lax.broadcasted_iota(jnp.int32, (PAIR, PAIR), 1)
    s_mask = (s_row < HEAD_DIM) == (s_col < HEAD_DIM)

    @pl.when(c == 0)
    def _():
        s0 = s0_ref[0]
        z = jnp.zeros((HEAD_DIM, HEAD_DIM), F32)
        s_c[...] = jnp.concatenate([jnp.concatenate([s0[0], z], axis=1),
                                    jnp.concatenate([z, s0[1]], axis=1)], axis=0)

    S = s_c[...]
    r = r_ref[0]
    lw = lw_ref[0]
    k = k_ref[0]
    v = v_ref[0]
    a = a_ref[0]
    b = b_ref[0]

    l1 = lw.astype(BF16)
    r1 = lw - l1.astype(F32)
    l2 = r1.astype(BF16)
    l3 = (r1 - l2.astype(F32)).astype(BF16)
    tri = tri_ref[...]
    cum = (jnp.dot(tri, l1, preferred_element_type=F32) + jnp.dot(tri, l2, preferred_element_type=F32)
           + jnp.dot(tri, l3, preferred_element_type=F32))
    cum_l = cum[L - 1:L, :]
    w_inc = jnp.exp(cum)
    w_inv = jnp.exp(-cum)
    at = a * jnp.exp(cum - lw)
    bt = b * w_inv
    kt = k * w_inv
    rt = r * w_inc
    dec = jnp.exp(cum_l - cum)

    row = lax.broadcasted_iota(jnp.int32, (L, L), 0)
    col = lax.broadcasted_iota(jnp.int32, (L, L), 1)
    strict = row > col
    incl = row >= col

    def per_head(x):
        return jnp.where(first, x, 0.0), jnp.where(first, 0.0, x)

    def merge(x1, x2):
        return jnp.where(first, x1, x2)

    at1, at2 = per_head(at)
    rt1, rt2 = per_head(rt)
    zero = jnp.zeros((L, L), F32)
    mab = [jnp.where(strict, _dot_nt(x, bt), zero) for x in (at1, at2)]
    mak = [jnp.where(strict, _dot_nt(x, kt), zero) for x in (at1, at2)]
    nrb = [jnp.where(incl, _dot_nt(x, bt), zero) for x in (rt1, rt2)]
    nrk = [jnp.where(incl, _dot_nt(x, kt), zero) for x in (rt1, rt2)]

    U = _dot_nt(at, S) + merge(_dot(mak[0], v), _dot(mak[1], v))
    n = 1
    while n < L:
        U = U + merge(_dot(mab[0], U), _dot(mab[1], U))
        n *= 2
        if n < L:
            mab = [_dot(m, m) for m in mab]
    y = (_dot_nt(rt, S) + merge(_dot(nrb[0], U), _dot(nrb[1], U))
         + merge(_dot(nrk[0], v), _dot(nrk[1], v)))
    y_out[0] = y

    s_new = S * jnp.exp(cum_l) + jnp.where(s_mask, _dot_tn(U, b * dec) + _dot_tn(v, k * dec), 0.0)
    s_c[...] = s_new
    s_out[0, 0] = s_new[0:HEAD_DIM, 0:HEAD_DIM]
    s_out[0, 1] = s_new[HEAD_DIM:PAIR, HEAD_DIM:PAIR]


def _wkv_call(r, lw, k, v, a, b, s0, chunk):
    bsz, seq, _ = r.shape
    grid = (bsz, N_PAIRS, seq // chunk)
    tok = pl.BlockSpec((1, chunk, PAIR), lambda bi, p, c: (bi, c, p))
    st = pl.BlockSpec((1, 2, HEAD_DIM, HEAD_DIM), lambda bi, p, c: (bi, p, 0, 0))
    tri = (jnp.arange(chunk)[:, None] >= jnp.arange(chunk)[None, :]).astype(BF16)
    return pl.pallas_call(
        _wkv_kernel, grid=grid,
        in_specs=[tok] * 6 + [st, pl.BlockSpec((chunk, chunk), lambda bi, p, c: (0, 0))],
        out_specs=[tok, st],
        out_shape=[jax.ShapeDtypeStruct((bsz, seq, D_RWKV), F32),
                   jax.ShapeDtypeStruct((bsz, HEADS, HEAD_DIM, HEAD_DIM), F32)],
        scratch_shapes=[pltpu.VMEM((PAIR, PAIR), F32)],
        compiler_params=pltpu.CompilerParams(
            dimension_semantics=("arbitrary", "arbitrary", "arbitrary")),
        name="wkv",
    )(r, lw, k, v, a, b, s0, tri)


def _post_kernel(y_ref, bonus_ref, g_ref, ma_ref, sgb_ref, x_ref, gnw_ref, gnb_ref, hsum_ref,
                 w_pb_ref, w_o_ref, n2g_ref, wq_hi_ref, wq_lo_ref,
                 x2_out, xn2_out, q_out):
    y = y_ref[...]
    hsum = hsum_ref[...]
    mean = _dot_hl(y, hsum) * (1.0 / HEAD_DIM)
    d = y - mean
    var = _dot_hl(d * d, hsum) * (1.0 / HEAD_DIM)
    yn = d * lax.rsqrt(var + GN_EPS) * gnw_ref[...] + gnb_ref[...] + bonus_ref[...]
    y_b = yn * g_ref[...]
    merged = ma_ref[...] + sgb_ref[...] * _dot(y_b, w_pb_ref[...])
    x2 = x_ref[...] + _dot(merged, w_o_ref[...])
    x2_out[...] = x2
    xn2 = _rms_norm(x2, n2g_ref[...])
    xn2_out[...] = xn2
    hi, lo = _split(xn2)
    wq_hi = wq_hi_ref[...]
    q_out[...] = (jnp.dot(hi, wq_hi, preferred_element_type=F32)
                  + jnp.dot(lo, wq_hi, preferred_element_type=F32)
                  + jnp.dot(hi, wq_lo_ref[...], preferred_element_type=F32))


def _post_call(y, bonus, g, ma, sgb, x, w, tt):
    n = y.shape[0]
    row = lambda i: (i, 0)
    const = lambda i: (0, 0)
    tok = lambda c: pl.BlockSpec((tt, c), row)
    weights = (w["gn_w"], w["gn_b"], w["hsum"], w["w_pb"], w["w_o"], w["norm2_g"], w["wq_hi"], w["wq_lo"])
    d_q = w["wq_hi"].shape[1]
    return pl.pallas_call(
        _post_kernel, grid=(n // tt,),
        in_specs=[tok(D_RWKV)] * 3 + [tok(D_MODEL)] * 3 + [pl.BlockSpec(a.shape, const) for a in weights],
        out_specs=[tok(D_MODEL), tok(D_MODEL), tok(d_q)],
        out_shape=[jax.ShapeDtypeStruct((n, D_MODEL), F32)] * 2 + [jax.ShapeDtypeStruct((n, d_q), F32)],
        compiler_params=pltpu.CompilerParams(dimension_semantics=("arbitrary",),
                                             vmem_limit_bytes=VMEM_LIMIT_BYTES),
        name="post",
    )(y, bonus, g, ma, sgb, x, *weights)


TOPK_HEADS = 4
STAIR_COUNTS = tuple(PEER_TOPK // (a + 1) for a in range(8))
STAIR_ROWS = 16 + 8 * 7 + 8


def _extract_max(s, iota, n_rows):
    m = jnp.max(s, axis=0, keepdims=True)
    idx = jnp.min(jnp.where(s == m, iota, n_rows), axis=0, keepdims=True)
    return m, idx, iota == idx


def _topk_kernel(q_ref, khi_ref, klo_ref, e_out, g_out, s_scr, v_scr, i_scr, c_scr, ci_scr, sc_scr):
    nt = lambda x, y: lax.dot_general(x, y, (((1,), (1,)), ((), ())), preferred_element_type=F32)
    for c in range(2 * TOPK_HEADS):
        h, p = divmod(c, 2)
        q_hi, q_lo = _split(q_ref[:, c * PEER_HALF:(c + 1) * PEER_HALF])
        k_hi = khi_ref[h, p]
        s_scr[c] = nt(k_hi, q_hi) + nt(k_hi, q_lo) + nt(klo_ref[h, p], q_hi)

    iota = lax.broadcasted_iota(jnp.int32, (PEER_KEYS, TOK_TILE), 0)

    def sub_key_step(j, carry):
        for c in range(2 * TOPK_HEADS):
            s = s_scr[c]
            m, idx, hit = _extract_max(s, iota, PEER_KEYS)
            v_scr[c, pl.ds(j, 1), :] = m
            i_scr[c, pl.ds(j, 1), :] = idx
            s_scr[c] = jnp.where(hit, -jnp.inf, s)
        return carry

    lax.fori_loop(0, PEER_TOPK, sub_key_step, 0)

    row8 = lax.broadcasted_iota(jnp.int32, (8, TOK_TILE), 0)
    for h in range(TOPK_HEADS):
        v1, i1 = v_scr[2 * h], i_scr[2 * h] * PEER_KEYS
        v2, i2 = v_scr[2 * h + 1], i_scr[2 * h + 1]
        vals = [v1[0:1] + v2]
        idxs = [i1[0:1] + i2]
        for a in range(1, 8):
            vals.append(jnp.where(row8 < STAIR_COUNTS[a], v1[a:a + 1] + v2[0:8], -jnp.inf))
            idxs.append(i1[a:a + 1] + i2[0:8])
        vals.append(v1[8:16] + v2[0:1])
        idxs.append(i1[8:16] + i2[0:1])
        c_scr[h] = jnp.concatenate(vals, axis=0)
        ci_scr[h] = jnp.concatenate(idxs, axis=0)

    iota_c = lax.broadcasted_iota(jnp.int32, (STAIR_ROWS, TOK_TILE), 0)

    def expert_step(j, carry):
        for h in range(TOPK_HEADS):
            s = c_scr[h]
            m, _, hit = _extract_max(s, iota_c, STAIR_ROWS)
            sc_scr[h, pl.ds(j, 1), :] = m
            e_out[pl.ds(h * PEER_TOPK + j, 1), :] = jnp.max(jnp.where(hit, ci_scr[h], -1), axis=0,
                                                             keepdims=True)
            c_scr[h] = jnp.where(hit, -jnp.inf, s)
        return carry

    lax.fori_loop(0, PEER_TOPK, expert_step, 0)

    for h in range(TOPK_HEADS):
        sc = sc_scr[h]
        e = jnp.exp(sc - sc[0:1])
        g_out[h * PEER_TOPK:(h + 1) * PEER_TOPK, :] = e / jnp.sum(e, axis=0, keepdims=True)


def _topk_call(q, khi, klo):
    n = q.shape[0]
    rows = TOPK_HEADS * PEER_TOPK
    sel = pl.BlockSpec((rows, TOK_TILE), lambda i, h: (h, i))
    keys = pl.BlockSpec((TOPK_HEADS, 2, PEER_KEYS, PEER_HALF), lambda i, h: (h, 0, 0, 0))
    chains = 2 * TOPK_HEADS
    return pl.pallas_call(
        _topk_kernel, grid=(n // TOK_TILE, PEER_HEADS // TOPK_HEADS),
        in_specs=[pl.BlockSpec((TOK_TILE, chains * PEER_HALF), lambda i, h: (i, h)), keys, keys],
        out_specs=[sel, sel],
        out_shape=[jax.ShapeDtypeStruct((PEER_SEL, n), jnp.int32), jax.ShapeDtypeStruct((PEER_SEL, n), F32)],
        scratch_shapes=[pltpu.VMEM((chains, PEER_KEYS, TOK_TILE), F32),
                        pltpu.VMEM((chains, PEER_TOPK, TOK_TILE), F32),
                        pltpu.VMEM((chains, PEER_TOPK, TOK_TILE), jnp.int32),
                        pltpu.VMEM((TOPK_HEADS, STAIR_ROWS, TOK_TILE), F32),
                        pltpu.VMEM((TOPK_HEADS, STAIR_ROWS, TOK_TILE), jnp.int32),
                        pltpu.VMEM((TOPK_HEADS, PEER_TOPK, TOK_TILE), F32)],
        compiler_params=pltpu.CompilerParams(dimension_semantics=("arbitrary", "arbitrary")),
        name="topk",
    )(q, khi, klo)


def _pack_kernel(u_ref, v_ref, w_out):
    ub = lax.bitcast_convert_type(u_ref[...].astype(BF16).astype(F32), jnp.uint32)
    vb = lax.bitcast_convert_type(v_ref[...].astype(BF16).astype(F32), jnp.uint32)
    w_out[...] = (ub & jnp.uint32(0xFFFF0000)) | (vb >> 16)


def _pack_call(u, v):
    n, d = u.shape
    rows = 512
    blk = pl.BlockSpec((rows, d), lambda i: (i, 0))
    return pl.pallas_call(
        _pack_kernel, grid=(n // rows,), in_specs=[blk, blk], out_specs=blk,
        out_shape=jax.ShapeDtypeStruct((n, d), jnp.uint32),
        compiler_params=pltpu.CompilerParams(dimension_semantics=("arbitrary",)),
        name="pack",
    )(u, v)


def _mix_tokens(plane, xn2_rows, gate_rows, x2_rows, nfg, between):
    n = len(xn2_rows)
    xs = [jnp.broadcast_to(xn2_rows[k], (8, D_MODEL)).astype(BF16) for k in range(n)]
    acts = [jnp.zeros((8, PEER_SEL), F32) for _ in range(n)]
    step = 0
    for s in range(8):
        for k in range(n):
            u = lax.bitcast_convert_type(plane(k, s) & jnp.uint32(0xFFFF0000), F32).astype(BF16)
            acts[k] = acts[k] + lax.dot_general(xs[k][:, s * 128:(s + 1) * 128], u,
                                                (((1,), (1,)), ((), ())), preferred_element_type=F32)
            between(step)
            step += 1
    coefs = []
    for k in range(n):
        act = acts[k]
        gelu = 0.5 * act * (1.0 + jnp.tanh(0.7978845608028654 * (act + 0.044715 * (act * act * act))))
        coefs.append((gate_rows[k] * gelu).astype(BF16))
    outs = [[] for _ in range(n)]
    for s in range(8):
        for k in range(n):
            v = lax.bitcast_convert_type(plane(k, s) << 16, F32).astype(BF16)
            mix = jnp.dot(coefs[k], v, preferred_element_type=F32)
            outs[k].append(x2_rows[k][:, s * 128:(s + 1) * 128] + mix[0:1])
            between(step)
            step += 1
    return [_rms_norm(jnp.concatenate(outs[k], axis=1), nfg) for k in range(n)]


def _peer_kernel(e_ref, g_ref, xn2_ref, x2_ref, nfg_ref, w_hbm, y_out,
                 e_smem, g_vmem, wbuf, sem_e, sem_w):
    to_smem = pltpu.make_async_copy(e_ref, e_smem, sem_e)
    to_smem.start()
    g_vmem[...] = g_ref[...].T
    to_smem.wait()

    def row_copy(t, slot, j):
        return pltpu.make_async_copy(w_hbm.at[e_smem[j, t]], wbuf.at[slot, :, j, :], sem_w.at[slot])

    def wait_slot(slot):
        pltpu.make_async_copy(wbuf.at[(slot + 1) % PEER_SLOTS], wbuf.at[slot], sem_w.at[slot]).wait()

    def mix_pair(t, slots, t_next, next_slots):
        starts = [(k, j) for k in range(2) for j in range(PEER_SEL)]
        per_step = len(starts) // 32

        def start_some(step):
            for k, j in starts[step * per_step:(step + 1) * per_step]:
                row_copy(t_next + k, next_slots[k], j).start(priority=j % 2)

        rows = lambda ref: [ref[pl.ds(t + k, 1), :] for k in range(2)]
        ys = _mix_tokens(lambda k, s: wbuf[slots[k], s], rows(xn2_ref), rows(g_vmem), rows(x2_ref),
                         nfg_ref[...], start_some)
        for k in range(2):
            y_out[pl.ds(t + k, 1), :] = ys[k]

    for j in range(PEER_SEL):
        row_copy(0, 0, j).start(priority=j % 2)
        row_copy(1, 1, j).start(priority=j % 2)

    def body(i, carry):
        t = PEER_SLOTS * i
        wait_slot(0)
        wait_slot(1)
        mix_pair(t, (0, 1), t + 2, (2, 3))
        wait_slot(2)
        wait_slot(3)
        mix_pair(t + 2, (2, 3), jnp.minimum(t + 4, TOK_TILE - 2), (0, 1))
        return carry

    lax.fori_loop(0, TOK_TILE // PEER_SLOTS, body, 0)
    wait_slot(0)
    wait_slot(1)


def _peer_call(eidx, gate, xn2, x2, nfg, w3, first_tile, n_tiles):
    sel = pl.BlockSpec((PEER_SEL, TOK_TILE), lambda i: (0, i + first_tile))
    tok = pl.BlockSpec((TOK_TILE, D_MODEL), lambda i: (i + first_tile, 0))
    return pl.pallas_call(
        _peer_kernel, grid=(n_tiles,),
        in_specs=[sel, sel, tok, tok, pl.BlockSpec((1, D_MODEL), lambda i: (0, 0)),
                  pl.BlockSpec(memory_space=pl.ANY)],
        out_specs=pl.BlockSpec((TOK_TILE, D_MODEL), lambda i: (i, 0)),
        out_shape=jax.ShapeDtypeStruct((n_tiles * TOK_TILE, D_MODEL), F32),
        scratch_shapes=[pltpu.SMEM((PEER_SEL, TOK_TILE), jnp.int32),
                        pltpu.VMEM((TOK_TILE, PEER_SEL), F32),
                        pltpu.VMEM((PEER_SLOTS, 8, PEER_SEL, 128), jnp.uint32),
                        pltpu.SemaphoreType.DMA,
                        pltpu.SemaphoreType.DMA((PEER_SLOTS,))],
        compiler_params=pltpu.CompilerParams(dimension_semantics=("arbitrary",)),
        name="peer",
    )(eidx, gate, xn2, x2, nfg, w3)


def _sc_gather_call(w3, idx):
    rows = idx.shape[0]
    per_worker = rows // (SC_CORES * SC_SUBCORES)
    mesh = plsc.VectorSubcoreMesh(core_axis_name="c", subcore_axis_name="s",
                                  num_cores=SC_CORES, num_subcores=SC_SUBCORES)

    @functools.partial(
        pl.kernel, mesh=mesh, out_type=jax.ShapeDtypeStruct((rows, 8, 128), jnp.uint32),
        scratch_types=[pltpu.VMEM((SC_WINDOW,), jnp.int32),
                       pltpu.VMEM((SC_WINDOW, 8, 128), jnp.uint32),
                       pltpu.SemaphoreType.DMA],
        name="sc_gather")
    def gather(table_hbm, idx_hbm, out_hbm, idx_v, rows_v, sem):
        base = (lax.axis_index("s") * SC_CORES + lax.axis_index("c")) * per_worker

        @pl.loop(0, per_worker // SC_WINDOW)
        def _(win):
            off = base + win * SC_WINDOW
            pltpu.sync_copy(idx_hbm.at[pl.ds(off, SC_WINDOW)], idx_v)
            pltpu.async_copy(table_hbm.at[idx_v], rows_v, sem).wait()
            pltpu.sync_copy(rows_v, out_hbm.at[pl.ds(off, SC_WINDOW)])

    return gather(w3, idx)


def _peer_staged_kernel(g_ref, xn2_ref, x2_ref, nfg_ref, w_ref, y_out, g_vmem):
    sub = pl.program_id(1)

    @pl.when(sub == 0)
    def _():
        g_vmem[...] = g_ref[...].T

    plane = lambda k, s: w_ref[pl.ds(k * (8 * PEER_SEL) + s, PEER_SEL, stride=8), :]
    rows = lambda ref: [ref[k:k + 1, :] for k in range(STAGE_TOKENS)]
    gates = [g_vmem[pl.ds(sub * STAGE_TOKENS + k, 1), :] for k in range(STAGE_TOKENS)]
    ys = _mix_tokens(plane, rows(xn2_ref), gates, rows(x2_ref), nfg_ref[...], lambda step: None)
    y_out[...] = jnp.concatenate(ys, axis=0)


def _peer_staged_call(gate, xn2, x2, nfg, staged, n_tiles):
    subs = TOK_TILE // STAGE_TOKENS
    tok = pl.BlockSpec((STAGE_TOKENS, D_MODEL), lambda i, j: (i * subs + j, 0))
    staged2d = staged.reshape(-1, 128)
    return pl.pallas_call(
        _peer_staged_kernel, grid=(n_tiles, subs),
        in_specs=[pl.BlockSpec((PEER_SEL, TOK_TILE), lambda i, j: (0, i)), tok, tok,
                  pl.BlockSpec((1, D_MODEL), lambda i, j: (0, 0)),
                  pl.BlockSpec((STAGE_TOKENS * PEER_SEL * 8, 128), lambda i, j: (i * subs + j, 0))],
        out_specs=tok,
        out_shape=jax.ShapeDtypeStruct((n_tiles * TOK_TILE, D_MODEL), F32),
        scratch_shapes=[pltpu.VMEM((TOK_TILE, PEER_SEL), F32)],
        compiler_params=pltpu.CompilerParams(dimension_semantics=("arbitrary", "arbitrary")),
        name="peer_staged",
    )(gate, xn2, x2, nfg, staged2d)


def _tile_choices(bsz, seq):
    tt = min(seq, 256)
    chunk = min(seq, 64)
    post = min(bsz * seq, 256)
    return tt, chunk, post


def _sc_tiles(tiles):
    return (tiles * SC_SHARE_PERCENT // 100) if tiles >= SC_MIN_TILES else 0


def _run_trunk(x, st_conv, st_shift, st_wkv, w):
    bsz, seq, _ = x.shape
    n = bsz * seq
    tt, chunk, post_tt = _tile_choices(bsz, seq)
    (r, lw, k, v, a, b, bonus, g, ma, sgb, new_shift, new_conv) = _pre_call(
        x, st_conv, st_shift.reshape(bsz, 1, D_MODEL), w, tt)
    y, new_wkv = _wkv_call(r, lw, k, v, a, b, st_wkv, chunk)
    flat = lambda t: t.reshape(n, t.shape[-1])
    x2, xn2, q = _post_call(flat(y), flat(bonus), flat(g), flat(ma), flat(sgb), flat(x), w, post_tt)
    eidx, gate = _topk_call(q, w["keys_hi"], w["keys_lo"])
    tiles = n // TOK_TILE
    sc_tiles = _sc_tiles(tiles)
    out = _peer_call(eidx, gate, xn2, x2, w["norm_f_g"], w["peer_w"], sc_tiles, tiles - sc_tiles)
    if sc_tiles:
        idx = eidx[:, :sc_tiles * TOK_TILE].T.reshape(-1)
        staged = _sc_gather_call(w["peer_w"], idx)
        out = jnp.concatenate([_peer_staged_call(gate, xn2, x2, w["norm_f_g"], staged, sc_tiles), out])
    return (out.reshape(bsz, seq, D_MODEL), new_conv[None], new_shift.reshape(1, bsz, D_MODEL),
            new_wkv[None])


def kernel(x_prompt, x_sample, state_conv, state_shift, state_wkv, norm1_g, w_in, conv_w, mu_rkv, mu_wag,
           w0, w1, w2, a0, a1, a2, g1, g2, k_k, k_a, r_k, gn_w, gn_b, w_pa, w_pb, w_o, norm2_g,
           peer_wq, peer_keys, peer_u, peer_v, norm_f_g):
    row = lambda t: t.reshape(1, -1)
    head = jnp.arange(D_RWKV) // HEAD_DIM
    wq_hi, wq_lo = _split(peer_wq[0])
    keys = peer_keys[0]
    keys_hi, keys_lo = _split(keys)
    w = dict(
        norm1_g=norm1_g, w_in=w_in[0].astype(BF16), conv_w=conv_w[0], mu_rkv=mu_rkv, mu_wag=mu_wag[0],
        w0=w0, w1=w1[0].astype(BF16), w2=w2[0].astype(BF16), a0=a0, a1=a1[0].astype(BF16),
        a2=a2[0].astype(BF16), g1=g1[0].astype(BF16), g2=g2[0].astype(BF16), k_k=k_k, k_a=k_a,
        r_k=row(r_k[0]), gn_w=gn_w, gn_b=gn_b, w_pa=w_pa[0].astype(BF16), w_pb=w_pb[0].astype(BF16),
        w_o=w_o[0].astype(BF16), norm2_g=norm2_g, wq_hi=wq_hi, wq_lo=wq_lo, keys_hi=keys_hi,
        keys_lo=keys_lo, norm_f_g=row(norm_f_g),
        peer_w=_pack_call(peer_u[0], peer_v[0]).reshape(-1, 8, 128),
        hsum=(head[:, None] == head[None, :]).astype(BF16),
    )
    bp = x_prompt.shape[0]
    zero_conv = jnp.zeros((bp, CONV_W - 1, D_CONV), F32)
    zero_shift = jnp.zeros((bp, D_MODEL), F32)
    zero_wkv = jnp.zeros((bp, HEADS, HEAD_DIM, HEAD_DIM), F32)
    y_s, conv_s, shift_s, wkv_s = _run_trunk(x_sample, state_conv[0], state_shift[0], state_wkv[0], w)
    y_p, conv_p, shift_p, wkv_p = _run_trunk(x_prompt, zero_conv, zero_shift, zero_wkv, w)
    return (y_p, y_s, conv_p, shift_p, wkv_p, conv_s, shift_s, wkv_s)
```

```python
import functools

import jax
import jax.numpy as jnp
from jax import lax
from jax.experimental import pallas as pl
from jax.experimental.pallas import tpu as pltpu
from jax.experimental.pallas import tpu_sc as plsc

F32 = jnp.float32
BF16 = jnp.bfloat16

D_MODEL = 1024
D_CONV = 512
CONV_W = 3
HEADS = 8
HEAD_DIM = 64
D_RWKV = HEADS * HEAD_DIM
PAIR = 2 * HEAD_DIM
N_PAIRS = HEADS // 2
GN_EPS = 64e-5
RMS_EPS = 1e-6
OFF_RKV = 3 * D_CONV
OFF_GATE = OFF_RKV + 3 * D_RWKV
D_IN = OFF_GATE + 2 * D_MODEL

PEER_HEADS = 8
PEER_KEYS = 128
PEER_HALF = 128
PEER_TOPK = 16
PEER_SEL = PEER_HEADS * PEER_TOPK
TOK_TILE = 128
PEER_SLOTS = 4
SC_CORES = 2
SC_SUBCORES = 16
SC_WINDOW = 64
SC_SHARE_PERCENT = 85
SC_CHUNKS = 4
SC_MIN_TILES = 32
STAGE_TOKENS = 8

VMEM_LIMIT_BYTES = 56 * 1024 * 1024


def _dot(a, b):
    return jnp.dot(a.astype(BF16), b.astype(BF16), preferred_element_type=F32)


def _dot_nt(a, b):
    return lax.dot_general(a.astype(BF16), b.astype(BF16), (((1,), (1,)), ((), ())),
                           preferred_element_type=F32)


def _dot_tn(a, b):
    return lax.dot_general(a.astype(BF16), b.astype(BF16), (((0,), (0,)), ((), ())),
                           preferred_element_type=F32)


def _split(a):
    hi = a.astype(BF16)
    lo = (a - hi.astype(F32)).astype(BF16)
    return hi, lo


def _dot_hl(a, w_bf16):
    hi, lo = _split(a)
    return (jnp.dot(hi, w_bf16, preferred_element_type=F32)
            + jnp.dot(lo, w_bf16, preferred_element_type=F32))


def _sigmoid(x):
    return 1.0 / (1.0 + jnp.exp(-x))


def _rms_norm(x, g):
    return x * lax.rsqrt(jnp.mean(x * x, axis=-1, keepdims=True) + RMS_EPS) * g


def _shift_rows(a, carry, n):
    rolled = pltpu.roll(a, n, 0)
    row = lax.broadcasted_iota(jnp.int32, a.shape, 0)
    for i in range(n):
        rolled = jnp.where(row == i, carry[i:i + 1], rolled)
    return rolled


def _pre_kernel(x_ref, conv0_ref, shift0_ref, n1g_ref, w_in_ref, convw_ref, mu_rkv_ref, mu_wag_ref,
                w0_ref, w1_ref, w2_ref, a0_ref, a1_ref, a2_ref, g1_ref, g2_ref, kk_ref, ka_ref, rk_ref,
                w_pa_ref, hsum_ref,
                r_out, lw_out, k_out, v_out, a_out, b_out, bonus_out, g_out, ma_out, sgb_out,
                nshift_out, nconv_out,
                xn_c, zrkv_c, u_c):
    t = pl.program_id(1)
    tt = x_ref.shape[1]
    xn = _rms_norm(x_ref[0], n1g_ref[...])
    xnb = xn.astype(BF16)

    @pl.when(t == 0)
    def _():
        prev = jnp.broadcast_to(shift0_ref[0], (8, D_MODEL))
        xn_c[...] = prev
        zrkv_c[...] = jnp.dot(prev.astype(BF16), w_in_ref[:, OFF_RKV:OFF_GATE],
                              preferred_element_type=F32)
        u_c[0:2, :] = conv0_ref[0]

    zbch = jnp.dot(xnb, w_in_ref[:, 0:OFF_RKV], preferred_element_type=F32)
    zb = zbch[:, 0:D_CONV]
    u = zbch[:, D_CONV:2 * D_CONV] * zbch[:, 2 * D_CONV:3 * D_CONV]
    u_prev = u_c[0:2, :]
    u1 = _shift_rows(u, u_prev[1:2], 1)
    u2 = _shift_rows(u, u_prev, 2)
    cw = convw_ref[...]
    y_a = zb * (cw[0:1] * u2 + cw[1:2] * u1 + cw[2:3] * u)
    u_last = u[tt - 2:tt, :]
    nconv_out[0] = u_last
    u_c[0:2, :] = u_last

    zg = jnp.dot(xnb, w_in_ref[:, OFF_GATE:D_IN], preferred_element_type=F32)
    ma_out[0] = _sigmoid(zg[:, 0:D_MODEL]) * _dot(y_a, w_pa_ref[...])
    sgb_out[0] = _sigmoid(zg[:, D_MODEL:2 * D_MODEL])

    zrkv = jnp.dot(xnb, w_in_ref[:, OFF_RKV:OFF_GATE], preferred_element_type=F32)
    zprev = _shift_rows(zrkv, zrkv_c[0:1, :], 1)
    zs = zrkv + mu_rkv_ref[...] * (zprev - zrkv)
    xprev = _shift_rows(xn, xn_c[0:1, :], 1)
    dx = xprev - xn
    mu = mu_wag_ref[...]
    xw = xn + dx * mu[0:1]
    xa = xn + dx * mu[1:2]
    xg = xn + dx * mu[2:3]
    xn_last = xn[tt - 1:tt, :]
    nshift_out[0] = xn_last
    xn_c[0:1, :] = xn_last
    zrkv_c[0:1, :] = zrkv[tt - 1:tt, :]

    wl = w0_ref[...] + _dot(jnp.tanh(_dot(xw, w1_ref[...])), w2_ref[...])
    softplus = jnp.maximum(-wl, 0.0) + jnp.log(1.0 + jnp.exp(-jnp.abs(wl)))
    lw_out[0] = -jnp.exp(-softplus - 0.5)
    a_sig = _sigmoid(a0_ref[...] + _dot(_dot(xa, a1_ref[...]), a2_ref[...]))
    g_out[0] = _dot(_sigmoid(_dot(xg, g1_ref[...])), g2_ref[...])

    r = zs[:, 0:D_RWKV]
    k = zs[:, D_RWKV:2 * D_RWKV]
    v = zs[:, 2 * D_RWKV:3 * D_RWKV]
    hsum = hsum_ref[...]
    kk = k * kk_ref[...]
    kk = kk / jnp.maximum(jnp.sqrt(_dot_hl(kk * kk, hsum)), 1e-12)
    k = k * (1.0 + (a_sig - 1.0) * ka_ref[...])
    r_out[0] = r
    k_out[0] = k
    v_out[0] = v
    a_out[0] = -kk
    b_out[0] = kk * a_sig
    bonus_out[0] = _dot_hl(r * k * rk_ref[...], hsum) * v


def _pre_call(x, conv0, shift0, w, tt):
    bsz, seq, _ = x.shape
    grid = (bsz, seq // tt)
    row = lambda b, t: (b, t, 0)
    per_b = lambda b, t: (b, 0, 0)
    const2 = lambda b, t: (0, 0)

    def tok(c):
        return pl.BlockSpec((1, tt, c), row)

    def full(a):
        return pl.BlockSpec(a.shape, const2)

    weights = (w["norm1_g"], w["w_in"], w["conv_w"], w["mu_rkv"], w["mu_wag"], w["w0"], w["w1"], w["w2"],
               w["a0"], w["a1"], w["a2"], w["g1"], w["g2"], w["k_k"], w["k_a"], w["r_k"], w["w_pa"],
               w["hsum"])
    in_specs = [tok(D_MODEL), pl.BlockSpec((1, CONV_W - 1, D_CONV), per_b),
                pl.BlockSpec((1, 1, D_MODEL), per_b)] + [full(a) for a in weights]
    tok_shape = lambda c: jax.ShapeDtypeStruct((bsz, seq, c), F32)
    out_shape = [tok_shape(D_RWKV)] * 8 + [tok_shape(D_MODEL)] * 2 + [
        jax.ShapeDtypeStruct((bsz, 1, D_MODEL), F32),
        jax.ShapeDtypeStruct((bsz, CONV_W - 1, D_CONV), F32)]
    out_specs = [tok(D_RWKV)] * 8 + [tok(D_MODEL)] * 2 + [
        pl.BlockSpec((1, 1, D_MODEL), per_b), pl.BlockSpec((1, CONV_W - 1, D_CONV), per_b)]
    return pl.pallas_call(
        _pre_kernel, grid=grid, in_specs=in_specs, out_specs=out_specs, out_shape=out_shape,
        scratch_shapes=[pltpu.VMEM((8, D_MODEL), F32), pltpu.VMEM((8, 3 * D_RWKV), F32),
                        pltpu.VMEM((8, D_CONV), F32)],
        compiler_params=pltpu.CompilerParams(dimension_semantics=("arbitrary", "arbitrary"),
                                             vmem_limit_bytes=VMEM_LIMIT_BYTES),
        name="pre",
    )(x, conv0, shift0, *weights)


def _wkv_kernel(r_ref, lw_ref, k_ref, v_ref, a_ref, b_ref, s0_ref, tri_ref,
                y_out, s_out, s_c):
    c = pl.program_id(1)
    L = r_ref.shape[1]
    pairs = range(N_PAIRS)
    lane = lax.broadcasted_iota(jnp.int32, (L, PAIR), 1)
    first = lane < HEAD_DIM
    s_row = lax.broadcasted_iota(jnp.int32, (PAIR, PAIR), 0)
    s_col = lax.broadcasted_iota(jnp.int32, (PAIR, PAIR), 1)
    s_mask = (s_row < HEAD_DIM) == (s_col < HEAD_DIM)

    @pl.when(c == 0)
    def _():
        z = jnp.zeros((HEAD_DIM, HEAD_DIM), F32)
        for p in pairs:
            s_c[p] = jnp.concatenate([jnp.concatenate([s0_ref[0, 2 * p], z], axis=1),
                                      jnp.concatenate([z, s0_ref[0, 2 * p + 1]], axis=1)], axis=0)

    def load(ref):
        return [ref[0, :, p * PAIR:(p + 1) * PAIR] for p in pairs]

    S = [s_c[p] for p in pairs]
    r, lw, k, v, a, b = (load(ref) for ref in (r_ref, lw_ref, k_ref, v_ref, a_ref, b_ref))

    tri = tri_ref[...]

    def cumsum_rows(x):
        l1 = x.astype(BF16)
        r1 = x - l1.astype(F32)
        l2 = r1.astype(BF16)
        l3 = (r1 - l2.astype(F32)).astype(BF16)
        return (jnp.dot(tri, l1, preferred_element_type=F32) + jnp.dot(tri, l2, preferred_element_type=F32)
                + jnp.dot(tri, l3, preferred_element_type=F32))

    cum = [cumsum_rows(x) for x in lw]
    cum_l = [x[L - 1:L, :] for x in cum]
    w_inv = [jnp.exp(-x) for x in cum]
    at = [a[p] * jnp.exp(cum[p] - lw[p]) for p in pairs]
    bt = [b[p] * w_inv[p] for p in pairs]
    kt = [k[p] * w_inv[p] for p in pairs]
    rt = [r[p] * jnp.exp(cum[p]) for p in pairs]
    dec = [jnp.exp(cum_l[p] - cum[p]) for p in pairs]

    row = lax.broadcasted_iota(jnp.int32, (L, L), 0)
    col = lax.broadcasted_iota(jnp.int32, (L, L), 1)
    strict = row > col
    incl = row >= col
    zero = jnp.zeros((L, L), F32)

    def per_head(x):
        return jnp.where(first, x, 0.0), jnp.where(first, 0.0, x)

    def merge(x1, x2):
        return jnp.where(first, x1, x2)

    def both(ms, x):
        return merge(_dot(ms[0], x), _dot(ms[1], x))

    at_h = [per_head(x) for x in at]
    rt_h = [per_head(x) for x in rt]
    mab = [[jnp.where(strict, _dot_nt(x, bt[p]), zero) for x in at_h[p]] for p in pairs]
    mak = [[jnp.where(strict, _dot_nt(x, kt[p]), zero) for x in at_h[p]] for p in pairs]
    nrb = [[jnp.where(incl, _dot_nt(x, bt[p]), zero) for x in rt_h[p]] for p in pairs]
    nrk = [[jnp.where(incl, _dot_nt(x, kt[p]), zero) for x in rt_h[p]] for p in pairs]

    U = [_dot_nt(at[p], S[p]) + both(mak[p], v[p]) for p in pairs]
    n = 1
    while n < L:
        U = [U[p] + both(mab[p], U[p]) for p in pairs]
        n *= 2
        if n < L:
            mab = [[_dot(m, m) for m in mab[p]] for p in pairs]
    for p in pairs:
        y_out[0, :, p * PAIR:(p + 1) * PAIR] = (_dot_nt(rt[p], S[p]) + both(nrb[p], U[p])
                                                + both(nrk[p], v[p]))
        s_new = S[p] * jnp.exp(cum_l[p]) + jnp.where(
            s_mask, _dot_tn(U[p], b[p] * dec[p]) + _dot_tn(v[p], k[p] * dec[p]), 0.0)
        s_c[p] = s_new
        s_out[0, 2 * p] = s_new[0:HEAD_DIM, 0:HEAD_DIM]
        s_out[0, 2 * p + 1] = s_new[HEAD_DIM:PAIR, HEAD_DIM:PAIR]


def _wkv_call(r, lw, k, v, a, b, s0, chunk):
    bsz, seq, _ = r.shape
    tok = pl.BlockSpec((1, chunk, D_RWKV), lambda bi, c: (bi, c, 0))
    st = pl.BlockSpec((1, HEADS, HEAD_DIM, HEAD_DIM), lambda bi, c: (bi, 0, 0, 0))
    tri = (jnp.arange(chunk)[:, None] >= jnp.arange(chunk)[None, :]).astype(BF16)
    return pl.pallas_call(
        _wkv_kernel, grid=(bsz, seq // chunk),
        in_specs=[tok] * 6 + [st, pl.BlockSpec((chunk, chunk), lambda bi, c: (0, 0))],
        out_specs=[tok, st],
        out_shape=[jax.ShapeDtypeStruct((bsz, seq, D_RWKV), F32),
                   jax.ShapeDtypeStruct((bsz, HEADS, HEAD_DIM, HEAD_DIM), F32)],
        scratch_shapes=[pltpu.VMEM((N_PAIRS, PAIR, PAIR), F32)],
        compiler_params=pltpu.CompilerParams(dimension_semantics=("arbitrary", "arbitrary")),
        name="wkv",
    )(r, lw, k, v, a, b, s0, tri)


def _post_kernel(y_ref, bonus_ref, g_ref, ma_ref, sgb_ref, x_ref, gnw_ref, gnb_ref, hsum_ref,
                 w_pb_ref, w_o_ref, n2g_ref, wq_hi_ref, wq_lo_ref,
                 x2_out, xn2_out, q_out):
    y = y_ref[...]
    hsum = hsum_ref[...]
    mean = _dot_hl(y, hsum) * (1.0 / HEAD_DIM)
    d = y - mean
    var = _dot_hl(d * d, hsum) * (1.0 / HEAD_DIM)
    yn = d * lax.rsqrt(var + GN_EPS) * gnw_ref[...] + gnb_ref[...] + bonus_ref[...]
    y_b = yn * g_ref[...]
    merged = ma_ref[...] + sgb_ref[...] * _dot(y_b, w_pb_ref[...])
    x2 = x_ref[...] + _dot(merged, w_o_ref[...])
    x2_out[...] = x2
    xn2 = _rms_norm(x2, n2g_ref[...])
    xn2_out[...] = xn2
    hi, lo = _split(xn2)
    wq_hi = wq_hi_ref[...]
    q_out[...] = (jnp.dot(hi, wq_hi, preferred_element_type=F32)
                  + jnp.dot(lo, wq_hi, preferred_element_type=F32)
                  + jnp.dot(hi, wq_lo_ref[...], preferred_element_type=F32))


def _post_call(y, bonus, g, ma, sgb, x, w, tt):
    n = y.shape[0]
    row = lambda i: (i, 0)
    const = lambda i: (0, 0)
    tok = lambda c: pl.BlockSpec((tt, c), row)
    weights = (w["gn_w"], w["gn_b"], w["hsum"], w["w_pb"], w["w_o"], w["norm2_g"], w["wq_hi"], w["wq_lo"])
    d_q = w["wq_hi"].shape[1]
    return pl.pallas_call(
        _post_kernel, grid=(n // tt,),
        in_specs=[tok(D_RWKV)] * 3 + [tok(D_MODEL)] * 3 + [pl.BlockSpec(a.shape, const) for a in weights],
        out_specs=[tok(D_MODEL), tok(D_MODEL), tok(d_q)],
        out_shape=[jax.ShapeDtypeStruct((n, D_MODEL), F32)] * 2 + [jax.ShapeDtypeStruct((n, d_q), F32)],
        compiler_params=pltpu.CompilerParams(dimension_semantics=("arbitrary",),
                                             vmem_limit_bytes=VMEM_LIMIT_BYTES),
        name="post",
    )(y, bonus, g, ma, sgb, x, *weights)


TOPK_HEADS = 4
STAIR_COUNTS = tuple(PEER_TOPK // (a + 1) for a in range(8))
STAIR_ROWS = 16 + 8 * 7 + 8


def _extract_max(s, iota, n_rows):
    m = jnp.max(s, axis=0, keepdims=True)
    idx = jnp.min(jnp.where(s == m, iota, n_rows), axis=0, keepdims=True)
    return m, idx, iota == idx


def _topk_kernel(q_ref, khi_ref, klo_ref, e_out, g_out, s_scr, v_scr, i_scr, c_scr, ci_scr, sc_scr):
    nt = lambda x, y: lax.dot_general(x, y, (((1,), (1,)), ((), ())), preferred_element_type=F32)
    for c in range(2 * TOPK_HEADS):
        h, p = divmod(c, 2)
        q_hi, q_lo = _split(q_ref[:, c * PEER_HALF:(c + 1) * PEER_HALF])
        k_hi = khi_ref[h, p]
        s_scr[c] = nt(k_hi, q_hi) + nt(k_hi, q_lo) + nt(klo_ref[h, p], q_hi)

    iota = lax.broadcasted_iota(jnp.int32, (PEER_KEYS, TOK_TILE), 0)

    def sub_key_step(j, carry):
        for c in range(2 * TOPK_HEADS):
            s = s_scr[c]
            m, idx, hit = _extract_max(s, iota, PEER_KEYS)
            v_scr[c, pl.ds(j, 1), :] = m
            i_scr[c, pl.ds(j, 1), :] = idx
            s_scr[c] = jnp.where(hit, -jnp.inf, s)
        return carry

    lax.fori_loop(0, PEER_TOPK, sub_key_step, 0)

    row8 = lax.broadcasted_iota(jnp.int32, (8, TOK_TILE), 0)
    for h in range(TOPK_HEADS):
        v1, i1 = v_scr[2 * h], i_scr[2 * h] * PEER_KEYS
        v2, i2 = v_scr[2 * h + 1], i_scr[2 * h + 1]
        vals = [v1[0:1] + v2]
        idxs = [i1[0:1] + i2]
        for a in range(1, 8):
            vals.append(jnp.where(row8 < STAIR_COUNTS[a], v1[a:a + 1] + v2[0:8], -jnp.inf))
            idxs.append(i1[a:a + 1] + i2[0:8])
        vals.append(v1[8:16] + v2[0:1])
        idxs.append(i1[8:16] + i2[0:1])
        c_scr[h] = jnp.concatenate(vals, axis=0)
        ci_scr[h] = jnp.concatenate(idxs, axis=0)

    iota_c = lax.broadcasted_iota(jnp.int32, (STAIR_ROWS, TOK_TILE), 0)

    def expert_step(j, carry):
        for h in range(TOPK_HEADS):
            s = c_scr[h]
            m, _, hit = _extract_max(s, iota_c, STAIR_ROWS)
            sc_scr[h, pl.ds(j, 1), :] = m
            e_out[pl.ds(h * PEER_TOPK + j, 1), :] = jnp.max(jnp.where(hit, ci_scr[h], -1), axis=0,
                                                             keepdims=True)
            c_scr[h] = jnp.where(hit, -jnp.inf, s)
        return carry

    lax.fori_loop(0, PEER_TOPK, expert_step, 0)

    for h in range(TOPK_HEADS):
        sc = sc_scr[h]
        e = jnp.exp(sc - sc[0:1])
        g_out[h * PEER_TOPK:(h + 1) * PEER_TOPK, :] = e / jnp.sum(e, axis=0, keepdims=True)


def _topk_call(q, khi, klo):
    n = q.shape[0]
    rows = TOPK_HEADS * PEER_TOPK
    sel = pl.BlockSpec((rows, TOK_TILE), lambda i, h: (h, i))
    keys = pl.BlockSpec((TOPK_HEADS, 2, PEER_KEYS, PEER_HALF), lambda i, h: (h, 0, 0, 0))
    chains = 2 * TOPK_HEADS
    return pl.pallas_call(
        _topk_kernel, grid=(n // TOK_TILE, PEER_HEADS // TOPK_HEADS),
        in_specs=[pl.BlockSpec((TOK_TILE, chains * PEER_HALF), lambda i, h: (i, h)), keys, keys],
        out_specs=[sel, sel],
        out_shape=[jax.ShapeDtypeStruct((PEER_SEL, n), jnp.int32), jax.ShapeDtypeStruct((PEER_SEL, n), F32)],
        scratch_shapes=[pltpu.VMEM((chains, PEER_KEYS, TOK_TILE), F32),
                        pltpu.VMEM((chains, PEER_TOPK, TOK_TILE), F32),
                        pltpu.VMEM((chains, PEER_TOPK, TOK_TILE), jnp.int32),
                        pltpu.VMEM((TOPK_HEADS, STAIR_ROWS, TOK_TILE), F32),
                        pltpu.VMEM((TOPK_HEADS, STAIR_ROWS, TOK_TILE), jnp.int32),
                        pltpu.VMEM((TOPK_HEADS, PEER_TOPK, TOK_TILE), F32)],
        compiler_params=pltpu.CompilerParams(dimension_semantics=("arbitrary", "arbitrary")),
        name="topk",
    )(q, khi, klo)


def _pack_kernel(u_ref, v_ref, w_out):
    ub = lax.bitcast_convert_type(u_ref[...].astype(BF16).astype(F32), jnp.uint32)
    vb = lax.bitcast_convert_type(v_ref[...].astype(BF16).astype(F32), jnp.uint32)
    w_out[...] = (ub & jnp.uint32(0xFFFF0000)) | (vb >> 16)


def _pack_call(u, v):
    n, d = u.shape
    rows = 512
    blk = pl.BlockSpec((rows, d), lambda i: (i, 0))
    return pl.pallas_call(
        _pack_kernel, grid=(n // rows,), in_specs=[blk, blk], out_specs=blk,
        out_shape=jax.ShapeDtypeStruct((n, d), jnp.uint32),
        compiler_params=pltpu.CompilerParams(dimension_semantics=("arbitrary",)),
        name="pack",
    )(u, v)


def _mix_tokens(plane, xn2_rows, gate_rows, x2_rows, nfg, between):
    n = len(xn2_rows)
    xs = [jnp.broadcast_to(xn2_rows[k], (8, D_MODEL)).astype(BF16) for k in range(n)]
    acts = [jnp.zeros((8, PEER_SEL), F32) for _ in range(n)]
    step = 0
    for s in range(8):
        for k in range(n):
            u = lax.bitcast_convert_type(plane(k, s) & jnp.uint32(0xFFFF0000), F32).astype(BF16)
            acts[k] = acts[k] + lax.dot_general(xs[k][:, s * 128:(s + 1) * 128], u,
                                                (((1,), (1,)), ((), ())), preferred_element_type=F32)
            between(step)
            step += 1
    coefs = []
    for k in range(n):
        act = acts[k]
        gelu = 0.5 * act * (1.0 + jnp.tanh(0.7978845608028654 * (act + 0.044715 * (act * act * act))))
        coefs.append((gate_rows[k] * gelu).astype(BF16))
    outs = [[] for _ in range(n)]
    for s in range(8):
        for k in range(n):
            v = lax.bitcast_convert_type(plane(k, s) << 16, F32).astype(BF16)
            mix = jnp.dot(coefs[k], v, preferred_element_type=F32)
            outs[k].append(x2_rows[k][:, s * 128:(s + 1) * 128] + mix[0:1])
            between(step)
            step += 1
    return [_rms_norm(jnp.concatenate(outs[k], axis=1), nfg) for k in range(n)]


def _peer_kernel(e_ref, g_ref, xn2_ref, x2_ref, nfg_ref, w_hbm, y_out,
                 e_smem, g_vmem, wbuf, sem_e, sem_w):
    to_smem = pltpu.make_async_copy(e_ref, e_smem, sem_e)
    to_smem.start()
    g_vmem[...] = g_ref[...].T
    to_smem.wait()

    def row_copy(t, slot, j):
        return pltpu.make_async_copy(w_hbm.at[e_smem[j, t]], wbuf.at[slot, :, j, :], sem_w.at[slot])

    def wait_slot(slot):
        pltpu.make_async_copy(wbuf.at[(slot + 1) % PEER_SLOTS], wbuf.at[slot], sem_w.at[slot]).wait()

    def mix_pair(t, slots, t_next, next_slots):
        starts = [(k, j) for k in range(2) for j in range(PEER_SEL)]
        per_step = len(starts) // 32

        def start_some(step):
            for k, j in starts[step * per_step:(step + 1) * per_step]:
                row_copy(t_next + k, next_slots[k], j).start(priority=j % 2)

        rows = lambda ref: [ref[pl.ds(t + k, 1), :] for k in range(2)]
        ys = _mix_tokens(lambda k, s: wbuf[slots[k], s], rows(xn2_ref), rows(g_vmem), rows(x2_ref),
                         nfg_ref[...], start_some)
        for k in range(2):
            y_out[pl.ds(t + k, 1), :] = ys[k]

    for j in range(PEER_SEL):
        row_copy(0, 0, j).start(priority=j % 2)
        row_copy(1, 1, j).start(priority=j % 2)

    def body(i, carry):
        t = PEER_SLOTS * i
        wait_slot(0)
        wait_slot(1)
        mix_pair(t, (0, 1), t + 2, (2, 3))
        wait_slot(2)
        wait_slot(3)
        mix_pair(t + 2, (2, 3), jnp.minimum(t + 4, TOK_TILE - 2), (0, 1))
        return carry

    lax.fori_loop(0, TOK_TILE // PEER_SLOTS, body, 0)
    wait_slot(0)
    wait_slot(1)


def _peer_call(eidx, gate, xn2, x2, nfg, w3, first_tile, n_tiles):
    sel = pl.BlockSpec((PEER_SEL, TOK_TILE), lambda i: (0, i + first_tile))
    tok = pl.BlockSpec((TOK_TILE, D_MODEL), lambda i: (i + first_tile, 0))
    return pl.pallas_call(
        _peer_kernel, grid=(n_tiles,),
        in_specs=[sel, sel, tok, tok, pl.BlockSpec((1, D_MODEL), lambda i: (0, 0)),
                  pl.BlockSpec(memory_space=pl.ANY)],
        out_specs=pl.BlockSpec((TOK_TILE, D_MODEL), lambda i: (i, 0)),
        out_shape=jax.ShapeDtypeStruct((n_tiles * TOK_TILE, D_MODEL), F32),
        scratch_shapes=[pltpu.SMEM((PEER_SEL, TOK_TILE), jnp.int32),
                        pltpu.VMEM((TOK_TILE, PEER_SEL), F32),
                        pltpu.VMEM((PEER_SLOTS, 8, PEER_SEL, 128), jnp.uint32),
                        pltpu.SemaphoreType.DMA,
                        pltpu.SemaphoreType.DMA((PEER_SLOTS,))],
        compiler_params=pltpu.CompilerParams(dimension_semantics=("arbitrary",)),
        name="peer",
    )(eidx, gate, xn2, x2, nfg, w3)


def _sc_gather_call(w3, idx):
    rows = idx.shape[0]
    per_worker = rows // (SC_CORES * SC_SUBCORES)
    mesh = plsc.VectorSubcoreMesh(core_axis_name="c", subcore_axis_name="s",
                                  num_cores=SC_CORES, num_subcores=SC_SUBCORES)

    @functools.partial(
        pl.kernel, mesh=mesh, out_type=jax.ShapeDtypeStruct((rows, 8, 128), jnp.uint32),
        scratch_types=[pltpu.VMEM((SC_WINDOW,), jnp.int32),
                       pltpu.VMEM((SC_WINDOW, 8, 128), jnp.uint32),
                       pltpu.SemaphoreType.DMA],
        name="sc_gather")
    def gather(table_hbm, idx_hbm, out_hbm, idx_v, rows_v, sem):
        base = (lax.axis_index("s") * SC_CORES + lax.axis_index("c")) * per_worker

        @pl.loop(0, per_worker // SC_WINDOW)
        def _(win):
            off = base + win * SC_WINDOW
            pltpu.sync_copy(idx_hbm.at[pl.ds(off, SC_WINDOW)], idx_v)
            pltpu.async_copy(table_hbm.at[idx_v], rows_v, sem).wait()
            pltpu.sync_copy(rows_v, out_hbm.at[pl.ds(off, SC_WINDOW)])

    return gather(w3, idx)


def _peer_staged_kernel(g_ref, xn2_ref, x2_ref, nfg_ref, w_ref, y_out, g_vmem):
    sub = pl.program_id(1)

    @pl.when(sub == 0)
    def _():
        g_vmem[...] = g_ref[...].T

    plane = lambda k, s: w_ref[pl.ds(k * (8 * PEER_SEL) + s, PEER_SEL, stride=8), :]
    rows = lambda ref: [ref[k:k + 1, :] for k in range(STAGE_TOKENS)]
    gates = [g_vmem[pl.ds(sub * STAGE_TOKENS + k, 1), :] for k in range(STAGE_TOKENS)]
    ys = _mix_tokens(plane, rows(xn2_ref), gates, rows(x2_ref), nfg_ref[...], lambda step: None)
    y_out[...] = jnp.concatenate(ys, axis=0)


def _peer_staged_call(gate, xn2, x2, nfg, staged, first_tile, n_tiles):
    subs = TOK_TILE // STAGE_TOKENS
    tok = pl.BlockSpec((STAGE_TOKENS, D_MODEL), lambda i, j: ((i + first_tile) * subs + j, 0))
    staged2d = staged.reshape(-1, 128)
    return pl.pallas_call(
        _peer_staged_kernel, grid=(n_tiles, subs),
        in_specs=[pl.BlockSpec((PEER_SEL, TOK_TILE), lambda i, j: (0, i + first_tile)), tok, tok,
                  pl.BlockSpec((1, D_MODEL), lambda i, j: (0, 0)),
                  pl.BlockSpec((STAGE_TOKENS * PEER_SEL * 8, 128), lambda i, j: (i * subs + j, 0))],
        out_specs=pl.BlockSpec((STAGE_TOKENS, D_MODEL), lambda i, j: (i * subs + j, 0)),
        out_shape=jax.ShapeDtypeStruct((n_tiles * TOK_TILE, D_MODEL), F32),
        scratch_shapes=[pltpu.VMEM((TOK_TILE, PEER_SEL), F32)],
        compiler_params=pltpu.CompilerParams(dimension_semantics=("arbitrary", "arbitrary")),
        name="peer_staged",
    )(gate, xn2, x2, nfg, staged2d)


def _tile_choices(bsz, seq):
    tt = min(seq, 256)
    chunk = min(seq, 64)
    post = min(bsz * seq, 256)
    return tt, chunk, post


def _sc_tiles(tiles):
    return (tiles * SC_SHARE_PERCENT // 100) if tiles >= SC_MIN_TILES else 0


def _run_trunk(x, st_conv, st_shift, st_wkv, w):
    bsz, seq, _ = x.shape
    n = bsz * seq
    tt, chunk, post_tt = _tile_choices(bsz, seq)
    (r, lw, k, v, a, b, bonus, g, ma, sgb, new_shift, new_conv) = _pre_call(
        x, st_conv, st_shift.reshape(bsz, 1, D_MODEL), w, tt)
    y, new_wkv = _wkv_call(r, lw, k, v, a, b, st_wkv, chunk)
    flat = lambda t: t.reshape(n, t.shape[-1])
    x2, xn2, q = _post_call(flat(y), flat(bonus), flat(g), flat(ma), flat(sgb), flat(x), w, post_tt)
    eidx, gate = _topk_call(q, w["keys_hi"], w["keys_lo"])
    tiles = n // TOK_TILE
    sc_tiles = _sc_tiles(tiles)
    out = _peer_call(eidx, gate, xn2, x2, w["norm_f_g"], w["peer_w"], sc_tiles, tiles - sc_tiles)
    if sc_tiles:
        pieces = []
        bounds = [sc_tiles * c // SC_CHUNKS for c in range(SC_CHUNKS + 1)]
        for lo, hi in zip(bounds[:-1], bounds[1:]):
            idx = eidx[:, lo * TOK_TILE:hi * TOK_TILE].T.reshape(-1)
            staged = _sc_gather_call(w["peer_w"], idx)
            pieces.append(_peer_staged_call(gate, xn2, x2, w["norm_f_g"], staged, lo, hi - lo))
        out = jnp.concatenate(pieces + [out])
    return (out.reshape(bsz, seq, D_MODEL), new_conv[None], new_shift.reshape(1, bsz, D_MODEL),
            new_wkv[None])


def kernel(x_prompt, x_sample, state_conv, state_shift, state_wkv, norm1_g, w_in, conv_w, mu_rkv, mu_wag,
           w0, w1, w2, a0, a1, a2, g1, g2, k_k, k_a, r_k, gn_w, gn_b, w_pa, w_pb, w_o, norm2_g,
           peer_wq, peer_keys, peer_u, peer_v, norm_f_g):
    row = lambda t: t.reshape(1, -1)
    head = jnp.arange(D_RWKV) // HEAD_DIM
    wq_hi, wq_lo = _split(peer_wq[0])
    keys = peer_keys[0]
    keys_hi, keys_lo = _split(keys)
    w = dict(
        norm1_g=norm1_g, w_in=w_in[0].astype(BF16), conv_w=conv_w[0], mu_rkv=mu_rkv, mu_wag=mu_wag[0],
        w0=w0, w1=w1[0].astype(BF16), w2=w2[0].astype(BF16), a0=a0, a1=a1[0].astype(BF16),
        a2=a2[0].astype(BF16), g1=g1[0].astype(BF16), g2=g2[0].astype(BF16), k_k=k_k, k_a=k_a,
        r_k=row(r_k[0]), gn_w=gn_w, gn_b=gn_b, w_pa=w_pa[0].astype(BF16), w_pb=w_pb[0].astype(BF16),
        w_o=w_o[0].astype(BF16), norm2_g=norm2_g, wq_hi=wq_hi, wq_lo=wq_lo, keys_hi=keys_hi,
        keys_lo=keys_lo, norm_f_g=row(norm_f_g),
        peer_w=_pack_call(peer_u[0], peer_v[0]).reshape(-1, 8, 128),
        hsum=(head[:, None] == head[None, :]).astype(BF16),
    )
    bp = x_prompt.shape[0]
    zero_conv = jnp.zeros((bp, CONV_W - 1, D_CONV), F32)
    zero_shift = jnp.zeros((bp, D_MODEL), F32)
    zero_wkv = jnp.zeros((bp, HEADS, HEAD_DIM, HEAD_DIM), F32)
    y_s, conv_s, shift_s, wkv_s = _run_trunk(x_sample, state_conv[0], state_shift[0], state_wkv[0], w)
    y_p, conv_p, shift_p, wkv_p = _run_trunk(x_prompt, zero_conv, zero_shift, zero_wkv, w)
    return (y_p, y_s, conv_p, shift_p, wkv_p, conv_s, shift_s, wkv_s)
```

```python
import functools

import jax
import jax.numpy as jnp
from jax import lax
from jax.experimental import pallas as pl
from jax.experimental.pallas import tpu as pltpu
from jax.experimental.pallas import tpu_sc as plsc

F32 = jnp.float32
BF16 = jnp.bfloat16

D_MODEL = 1024
D_CONV = 512
CONV_W = 3
HEADS = 8
HEAD_DIM = 64
D_RWKV = HEADS * HEAD_DIM
PAIR = 2 * HEAD_DIM
N_PAIRS = HEADS // 2
GN_EPS = 64e-5
RMS_EPS = 1e-6
OFF_RKV = 3 * D_CONV
OFF_GATE = OFF_RKV + 3 * D_RWKV
D_IN = OFF_GATE + 2 * D_MODEL

PEER_HEADS = 8
PEER_KEYS = 128
PEER_HALF = 128
PEER_TOPK = 16
PEER_SEL = PEER_HEADS * PEER_TOPK
TOK_TILE = 128
PEER_SLOTS = 4
SC_CORES = 2
SC_SUBCORES = 16
SC_WINDOW = 64
SC_SHARE_PERCENT = 94
SC_CHUNKS = 2
PROMPT_GROUPS = 4
SC_MIN_TILES = 32
STAGE_TOKENS = 8

VMEM_LIMIT_BYTES = 56 * 1024 * 1024


def _dot(a, b):
    return jnp.dot(a.astype(BF16), b.astype(BF16), preferred_element_type=F32)


def _dot_nt(a, b):
    return lax.dot_general(a.astype(BF16), b.astype(BF16), (((1,), (1,)), ((), ())),
                           preferred_element_type=F32)


def _dot_tn(a, b):
    return lax.dot_general(a.astype(BF16), b.astype(BF16), (((0,), (0,)), ((), ())),
                           preferred_element_type=F32)


def _split(a):
    hi = a.astype(BF16)
    lo = (a - hi.astype(F32)).astype(BF16)
    return hi, lo


def _dot_hl(a, w_bf16):
    hi, lo = _split(a)
    return (jnp.dot(hi, w_bf16, preferred_element_type=F32)
            + jnp.dot(lo, w_bf16, preferred_element_type=F32))


def _sigmoid(x):
    return 1.0 / (1.0 + jnp.exp(-x))


def _rms_norm(x, g):
    return x * lax.rsqrt(jnp.mean(x * x, axis=-1, keepdims=True) + RMS_EPS) * g


def _shift_rows(a, carry, n):
    rolled = pltpu.roll(a, n, 0)
    row = lax.broadcasted_iota(jnp.int32, a.shape, 0)
    for i in range(n):
        rolled = jnp.where(row == i, carry[i:i + 1], rolled)
    return rolled


def _pre_kernel(x_ref, conv0_ref, shift0_ref, n1g_ref, w_in_ref, convw_ref, mu_rkv_ref, mu_wag_ref,
                w0_ref, w1_ref, w2_ref, a0_ref, a1_ref, a2_ref, g1_ref, g2_ref, kk_ref, ka_ref, rk_ref,
                w_pa_ref, hsum_ref,
                r_out, lw_out, k_out, v_out, a_out, b_out, bonus_out, g_out, ma_out, sgb_out,
                nshift_out, nconv_out,
                xn_c, zrkv_c, u_c):
    t = pl.program_id(1)
    tt = x_ref.shape[1]
    xn = _rms_norm(x_ref[0], n1g_ref[...])
    xnb = xn.astype(BF16)

    @pl.when(t == 0)
    def _():
        prev = jnp.broadcast_to(shift0_ref[0], (8, D_MODEL))
        xn_c[...] = prev
        zrkv_c[...] = jnp.dot(prev.astype(BF16), w_in_ref[:, OFF_RKV:OFF_GATE],
                              preferred_element_type=F32)
        u_c[0:2, :] = conv0_ref[0]

    zbch = jnp.dot(xnb, w_in_ref[:, 0:OFF_RKV], preferred_element_type=F32)
    zb = zbch[:, 0:D_CONV]
    u = zbch[:, D_CONV:2 * D_CONV] * zbch[:, 2 * D_CONV:3 * D_CONV]
    u_prev = u_c[0:2, :]
    u1 = _shift_rows(u, u_prev[1:2], 1)
    u2 = _shift_rows(u, u_prev, 2)
    cw = convw_ref[...]
    y_a = zb * (cw[0:1] * u2 + cw[1:2] * u1 + cw[2:3] * u)
    u_last = u[tt - 2:tt, :]
    nconv_out[0] = u_last
    u_c[0:2, :] = u_last

    zg = jnp.dot(xnb, w_in_ref[:, OFF_GATE:D_IN], preferred_element_type=F32)
    ma_out[0] = _sigmoid(zg[:, 0:D_MODEL]) * _dot(y_a, w_pa_ref[...])
    sgb_out[0] = _sigmoid(zg[:, D_MODEL:2 * D_MODEL])

    zrkv = jnp.dot(xnb, w_in_ref[:, OFF_RKV:OFF_GATE], preferred_element_type=F32)
    zprev = _shift_rows(zrkv, zrkv_c[0:1, :], 1)
    zs = zrkv + mu_rkv_ref[...] * (zprev - zrkv)
    xprev = _shift_rows(xn, xn_c[0:1, :], 1)
    dx = xprev - xn
    mu = mu_wag_ref[...]
    xw = xn + dx * mu[0:1]
    xa = xn + dx * mu[1:2]
    xg = xn + dx * mu[2:3]
    xn_last = xn[tt - 1:tt, :]
    nshift_out[0] = xn_last
    xn_c[0:1, :] = xn_last
    zrkv_c[0:1, :] = zrkv[tt - 1:tt, :]

    wl = w0_ref[...] + _dot(jnp.tanh(_dot(xw, w1_ref[...])), w2_ref[...])
    softplus = jnp.maximum(-wl, 0.0) + jnp.log(1.0 + jnp.exp(-jnp.abs(wl)))
    lw_out[0] = -jnp.exp(-softplus - 0.5)
    a_sig = _sigmoid(a0_ref[...] + _dot(_dot(xa, a1_ref[...]), a2_ref[...]))
    g_out[0] = _dot(_sigmoid(_dot(xg, g1_ref[...])), g2_ref[...])

    r = zs[:, 0:D_RWKV]
    k = zs[:, D_RWKV:2 * D_RWKV]
    v = zs[:, 2 * D_RWKV:3 * D_RWKV]
    hsum = hsum_ref[...]
    kk = k * kk_ref[...]
    kk = kk / jnp.maximum(jnp.sqrt(_dot_hl(kk * kk, hsum)), 1e-12)
    k = k * (1.0 + (a_sig - 1.0) * ka_ref[...])
    r_out[0] = r
    k_out[0] = k
    v_out[0] = v
    a_out[0] = -kk
    b_out[0] = kk * a_sig
    bonus_out[0] = _dot_hl(r * k * rk_ref[...], hsum) * v


def _pre_call(x, conv0, shift0, w, tt):
    bsz, seq, _ = x.shape
    grid = (bsz, seq // tt)
    row = lambda b, t: (b, t, 0)
    per_b = lambda b, t: (b, 0, 0)
    const2 = lambda b, t: (0, 0)

    def tok(c):
        return pl.BlockSpec((1, tt, c), row)

    def full(a):
        return pl.BlockSpec(a.shape, const2)

    weights = (w["norm1_g"], w["w_in"], w["conv_w"], w["mu_rkv"], w["mu_wag"], w["w0"], w["w1"], w["w2"],
               w["a0"], w["a1"], w["a2"], w["g1"], w["g2"], w["k_k"], w["k_a"], w["r_k"], w["w_pa"],
               w["hsum"])
    in_specs = [tok(D_MODEL), pl.BlockSpec((1, CONV_W - 1, D_CONV), per_b),
                pl.BlockSpec((1, 1, D_MODEL), per_b)] + [full(a) for a in weights]
    tok_shape = lambda c: jax.ShapeDtypeStruct((bsz, seq, c), F32)
    out_shape = [tok_shape(D_RWKV)] * 8 + [tok_shape(D_MODEL)] * 2 + [
        jax.ShapeDtypeStruct((bsz, 1, D_MODEL), F32),
        jax.ShapeDtypeStruct((bsz, CONV_W - 1, D_CONV), F32)]
    out_specs = [tok(D_RWKV)] * 8 + [tok(D_MODEL)] * 2 + [
        pl.BlockSpec((1, 1, D_MODEL), per_b), pl.BlockSpec((1, CONV_W - 1, D_CONV), per_b)]
    return pl.pallas_call(
        _pre_kernel, grid=grid, in_specs=in_specs, out_specs=out_specs, out_shape=out_shape,
        scratch_shapes=[pltpu.VMEM((8, D_MODEL), F32), pltpu.VMEM((8, 3 * D_RWKV), F32),
                        pltpu.VMEM((8, D_CONV), F32)],
        compiler_params=pltpu.CompilerParams(dimension_semantics=("arbitrary", "arbitrary"),
                                             vmem_limit_bytes=VMEM_LIMIT_BYTES),
        name="pre",
    )(x, conv0, shift0, *weights)


def _wkv_kernel(r_ref, lw_ref, k_ref, v_ref, a_ref, b_ref, s0_ref, tri_ref,
                y_out, s_out, s_c):
    c = pl.program_id(1)
    L = r_ref.shape[1]
    pairs = range(N_PAIRS)
    lane = lax.broadcasted_iota(jnp.int32, (L, PAIR), 1)
    first = lane < HEAD_DIM
    s_row = lax.broadcasted_iota(jnp.int32, (PAIR, PAIR), 0)
    s_col = lax.broadcasted_iota(jnp.int32, (PAIR, PAIR), 1)
    s_mask = (s_row < HEAD_DIM) == (s_col < HEAD_DIM)

    @pl.when(c == 0)
    def _():
        z = jnp.zeros((HEAD_DIM, HEAD_DIM), F32)
        for p in pairs:
            s_c[p] = jnp.concatenate([jnp.concatenate([s0_ref[0, 2 * p], z], axis=1),
                                      jnp.concatenate([z, s0_ref[0, 2 * p + 1]], axis=1)], axis=0)

    def load(ref):
        return [ref[0, :, p * PAIR:(p + 1) * PAIR] for p in pairs]

    S = [s_c[p] for p in pairs]
    r, lw, k, v, a, b = (load(ref) for ref in (r_ref, lw_ref, k_ref, v_ref, a_ref, b_ref))

    tri = tri_ref[...]

    def cumsum_rows(x):
        l1 = x.astype(BF16)
        r1 = x - l1.astype(F32)
        l2 = r1.astype(BF16)
        l3 = (r1 - l2.astype(F32)).astype(BF16)
        return (jnp.dot(tri, l1, preferred_element_type=F32) + jnp.dot(tri, l2, preferred_element_type=F32)
                + jnp.dot(tri, l3, preferred_element_type=F32))

    cum = [cumsum_rows(x) for x in lw]
    cum_l = [x[L - 1:L, :] for x in cum]
    w_inv = [jnp.exp(-x) for x in cum]
    at = [a[p] * jnp.exp(cum[p] - lw[p]) for p in pairs]
    bt = [b[p] * w_inv[p] for p in pairs]
    kt = [k[p] * w_inv[p] for p in pairs]
    rt = [r[p] * jnp.exp(cum[p]) for p in pairs]
    dec = [jnp.exp(cum_l[p] - cum[p]) for p in pairs]

    row = lax.broadcasted_iota(jnp.int32, (L, L), 0)
    col = lax.broadcasted_iota(jnp.int32, (L, L), 1)
    strict = row > col
    incl = row >= col
    zero = jnp.zeros((L, L), F32)

    def per_head(x):
        return jnp.where(first, x, 0.0), jnp.where(first, 0.0, x)

    def merge(x1, x2):
        return jnp.where(first, x1, x2)

    def both(ms, x):
        return merge(_dot(ms[0], x), _dot(ms[1], x))

    at_h = [per_head(x) for x in at]
    rt_h = [per_head(x) for x in rt]
    mab = [[jnp.where(strict, _dot_nt(x, bt[p]), zero) for x in at_h[p]] for p in pairs]
    mak = [[jnp.where(strict, _dot_nt(x, kt[p]), zero) for x in at_h[p]] for p in pairs]
    nrb = [[jnp.where(incl, _dot_nt(x, bt[p]), zero) for x in rt_h[p]] for p in pairs]
    nrk = [[jnp.where(incl, _dot_nt(x, kt[p]), zero) for x in rt_h[p]] for p in pairs]

    U = [_dot_nt(at[p], S[p]) + both(mak[p], v[p]) for p in pairs]
    n = 1
    while n < L:
        U = [U[p] + both(mab[p], U[p]) for p in pairs]
        n *= 2
        if n < L:
            mab = [[_dot(m, m) for m in mab[p]] for p in pairs]
    for p in pairs:
        y_out[0, :, p * PAIR:(p + 1) * PAIR] = (_dot_nt(rt[p], S[p]) + both(nrb[p], U[p])
                                                + both(nrk[p], v[p]))
        s_new = S[p] * jnp.exp(cum_l[p]) + jnp.where(
            s_mask, _dot_tn(U[p], b[p] * dec[p]) + _dot_tn(v[p], k[p] * dec[p]), 0.0)
        s_c[p] = s_new
        s_out[0, 2 * p] = s_new[0:HEAD_DIM, 0:HEAD_DIM]
        s_out[0, 2 * p + 1] = s_new[HEAD_DIM:PAIR, HEAD_DIM:PAIR]


def _wkv_call(r, lw, k, v, a, b, s0, chunk):
    bsz, seq, _ = r.shape
    tok = pl.BlockSpec((1, chunk, D_RWKV), lambda bi, c: (bi, c, 0))
    st = pl.BlockSpec((1, HEADS, HEAD_DIM, HEAD_DIM), lambda bi, c: (bi, 0, 0, 0))
    tri = (jnp.arange(chunk)[:, None] >= jnp.arange(chunk)[None, :]).astype(BF16)
    return pl.pallas_call(
        _wkv_kernel, grid=(bsz, seq // chunk),
        in_specs=[tok] * 6 + [st, pl.BlockSpec((chunk, chunk), lambda bi, c: (0, 0))],
        out_specs=[tok, st],
        out_shape=[jax.ShapeDtypeStruct((bsz, seq, D_RWKV), F32),
                   jax.ShapeDtypeStruct((bsz, HEADS, HEAD_DIM, HEAD_DIM), F32)],
        scratch_shapes=[pltpu.VMEM((N_PAIRS, PAIR, PAIR), F32)],
        compiler_params=pltpu.CompilerParams(dimension_semantics=("arbitrary", "arbitrary")),
        name="wkv",
    )(r, lw, k, v, a, b, s0, tri)


def _post_kernel(y_ref, bonus_ref, g_ref, ma_ref, sgb_ref, x_ref, gnw_ref, gnb_ref, hsum_ref,
                 w_pb_ref, w_o_ref, n2g_ref, wq_hi_ref, wq_lo_ref,
                 x2_out, xn2_out, q_out):
    y = y_ref[...]
    hsum = hsum_ref[...]
    mean = _dot_hl(y, hsum) * (1.0 / HEAD_DIM)
    d = y - mean
    var = _dot_hl(d * d, hsum) * (1.0 / HEAD_DIM)
    yn = d * lax.rsqrt(var + GN_EPS) * gnw_ref[...] + gnb_ref[...] + bonus_ref[...]
    y_b = yn * g_ref[...]
    merged = ma_ref[...] + sgb_ref[...] * _dot(y_b, w_pb_ref[...])
    x2 = x_ref[...] + _dot(merged, w_o_ref[...])
    x2_out[...] = x2
    xn2 = _rms_norm(x2, n2g_ref[...])
    xn2_out[...] = xn2
    hi, lo = _split(xn2)
    wq_hi = wq_hi_ref[...]
    q_out[...] = (jnp.dot(hi, wq_hi, preferred_element_type=F32)
                  + jnp.dot(lo, wq_hi, preferred_element_type=F32)
                  + jnp.dot(hi, wq_lo_ref[...], preferred_element_type=F32))


def _post_call(y, bonus, g, ma, sgb, x, w, tt):
    n = y.shape[0]
    row = lambda i: (i, 0)
    const = lambda i: (0, 0)
    tok = lambda c: pl.BlockSpec((tt, c), row)
    weights = (w["gn_w"], w["gn_b"], w["hsum"], w["w_pb"], w["w_o"], w["norm2_g"], w["wq_hi"], w["wq_lo"])
    d_q = w["wq_hi"].shape[1]
    return pl.pallas_call(
        _post_kernel, grid=(n // tt,),
        in_specs=[tok(D_RWKV)] * 3 + [tok(D_MODEL)] * 3 + [pl.BlockSpec(a.shape, const) for a in weights],
        out_specs=[tok(D_MODEL), tok(D_MODEL), tok(d_q)],
        out_shape=[jax.ShapeDtypeStruct((n, D_MODEL), F32)] * 2 + [jax.ShapeDtypeStruct((n, d_q), F32)],
        compiler_params=pltpu.CompilerParams(dimension_semantics=("arbitrary",),
                                             vmem_limit_bytes=VMEM_LIMIT_BYTES),
        name="post",
    )(y, bonus, g, ma, sgb, x, *weights)


TOPK_HEADS = 4
STAIR_COUNTS = tuple(PEER_TOPK // (a + 1) for a in range(8))
STAIR_ROWS = 16 + 8 * 7 + 8


def _extract_max(s, iota, n_rows):
    m = jnp.max(s, axis=0, keepdims=True)
    idx = jnp.min(jnp.where(s == m, iota, n_rows), axis=0, keepdims=True)
    return m, idx, iota == idx


def _topk_kernel(q_ref, khi_ref, klo_ref, e_out, g_out, s_scr, v_scr, i_scr, c_scr, ci_scr, sc_scr):
    nt = lambda x, y: lax.dot_general(x, y, (((1,), (1,)), ((), ())), preferred_element_type=F32)
    for c in range(2 * TOPK_HEADS):
        h, p = divmod(c, 2)
        q_hi, q_lo = _split(q_ref[:, c * PEER_HALF:(c + 1) * PEER_HALF])
        k_hi = khi_ref[h, p]
        s_scr[c] = nt(k_hi, q_hi) + nt(k_hi, q_lo) + nt(klo_ref[h, p], q_hi)

    iota = lax.broadcasted_iota(jnp.int32, (PEER_KEYS, TOK_TILE), 0)

    def sub_key_step(j, carry):
        for c in range(2 * TOPK_HEADS):
            s = s_scr[c]
            m, idx, hit = _extract_max(s, iota, PEER_KEYS)
            v_scr[c, pl.ds(j, 1), :] = m
            i_scr[c, pl.ds(j, 1), :] = idx
            s_scr[c] = jnp.where(hit, -jnp.inf, s)
        return carry

    lax.fori_loop(0, PEER_TOPK, sub_key_step, 0)

    row8 = lax.broadcasted_iota(jnp.int32, (8, TOK_TILE), 0)
    for h in range(TOPK_HEADS):
        v1, i1 = v_scr[2 * h], i_scr[2 * h] * PEER_KEYS
        v2, i2 = v_scr[2 * h + 1], i_scr[2 * h + 1]
        vals = [v1[0:1] + v2]
        idxs = [i1[0:1] + i2]
        for a in range(1, 8):
            vals.append(jnp.where(row8 < STAIR_COUNTS[a], v1[a:a + 1] + v2[0:8], -jnp.inf))
            idxs.append(i1[a:a + 1] + i2[0:8])
        vals.append(v1[8:16] + v2[0:1])
        idxs.append(i1[8:16] + i2[0:1])
        c_scr[h] = jnp.concatenate(vals, axis=0)
        ci_scr[h] = jnp.concatenate(idxs, axis=0)

    iota_c = lax.broadcasted_iota(jnp.int32, (STAIR_ROWS, TOK_TILE), 0)

    def expert_step(j, carry):
        for h in range(TOPK_HEADS):
            s = c_scr[h]
            m, _, hit = _extract_max(s, iota_c, STAIR_ROWS)
            sc_scr[h, pl.ds(j, 1), :] = m
            e_out[pl.ds(h * PEER_TOPK + j, 1), :] = jnp.max(jnp.where(hit, ci_scr[h], -1), axis=0,
                                                             keepdims=True)
            c_scr[h] = jnp.where(hit, -jnp.inf, s)
        return carry

    lax.fori_loop(0, PEER_TOPK, expert_step, 0)

    for h in range(TOPK_HEADS):
        sc = sc_scr[h]
        e = jnp.exp(sc - sc[0:1])
        g_out[h * PEER_TOPK:(h + 1) * PEER_TOPK, :] = e / jnp.sum(e, axis=0, keepdims=True)


def _topk_call(q, khi, klo):
    n = q.shape[0]
    rows = TOPK_HEADS * PEER_TOPK
    sel = pl.BlockSpec((rows, TOK_TILE), lambda i, h: (h, i))
    keys = pl.BlockSpec((TOPK_HEADS, 2, PEER_KEYS, PEER_HALF), lambda i, h: (h, 0, 0, 0))
    chains = 2 * TOPK_HEADS
    return pl.pallas_call(
        _topk_kernel, grid=(n // TOK_TILE, PEER_HEADS // TOPK_HEADS),
        in_specs=[pl.BlockSpec((TOK_TILE, chains * PEER_HALF), lambda i, h: (i, h)), keys, keys],
        out_specs=[sel, sel],
        out_shape=[jax.ShapeDtypeStruct((PEER_SEL, n), jnp.int32), jax.ShapeDtypeStruct((PEER_SEL, n), F32)],
        scratch_shapes=[pltpu.VMEM((chains, PEER_KEYS, TOK_TILE), F32),
                        pltpu.VMEM((chains, PEER_TOPK, TOK_TILE), F32),
                        pltpu.VMEM((chains, PEER_TOPK, TOK_TILE), jnp.int32),
                        pltpu.VMEM((TOPK_HEADS, STAIR_ROWS, TOK_TILE), F32),
                        pltpu.VMEM((TOPK_HEADS, STAIR_ROWS, TOK_TILE), jnp.int32),
                        pltpu.VMEM((TOPK_HEADS, PEER_TOPK, TOK_TILE), F32)],
        compiler_params=pltpu.CompilerParams(dimension_semantics=("arbitrary", "arbitrary")),
        name="topk",
    )(q, khi, klo)


def _pack_kernel(u_ref, v_ref, w_out):
    ub = lax.bitcast_convert_type(u_ref[...].astype(BF16).astype(F32), jnp.uint32)
    vb = lax.bitcast_convert_type(v_ref[...].astype(BF16).astype(F32), jnp.uint32)
    w_out[...] = (ub & jnp.uint32(0xFFFF0000)) | (vb >> 16)


def _pack_call(u, v):
    n, d = u.shape
    rows = 512
    blk = pl.BlockSpec((rows, d), lambda i: (i, 0))
    return pl.pallas_call(
        _pack_kernel, grid=(n // rows,), in_specs=[blk, blk], out_specs=blk,
        out_shape=jax.ShapeDtypeStruct((n, d), jnp.uint32),
        compiler_params=pltpu.CompilerParams(dimension_semantics=("arbitrary",)),
        name="pack",
    )(u, v)


def _mix_tokens(plane, xn2_rows, gate_rows, x2_rows, nfg, between):
    n = len(xn2_rows)
    xs = [jnp.broadcast_to(xn2_rows[k], (8, D_MODEL)).astype(BF16) for k in range(n)]
    acts = [jnp.zeros((8, PEER_SEL), F32) for _ in range(n)]
    step = 0
    for s in range(8):
        for k in range(n):
            u = lax.bitcast_convert_type(plane(k, s) & jnp.uint32(0xFFFF0000), F32).astype(BF16)
            acts[k] = acts[k] + lax.dot_general(xs[k][:, s * 128:(s + 1) * 128], u,
                                                (((1,), (1,)), ((), ())), preferred_element_type=F32)
            between(step)
            step += 1
    coefs = []
    for k in range(n):
        act = acts[k]
        gelu = 0.5 * act * (1.0 + jnp.tanh(0.7978845608028654 * (act + 0.044715 * (act * act * act))))
        coefs.append((gate_rows[k] * gelu).astype(BF16))
    outs = [[] for _ in range(n)]
    for s in range(8):
        for k in range(n):
            v = lax.bitcast_convert_type(plane(k, s) << 16, F32).astype(BF16)
            mix = jnp.dot(coefs[k], v, preferred_element_type=F32)
            outs[k].append(x2_rows[k][:, s * 128:(s + 1) * 128] + mix[0:1])
            between(step)
            step += 1
    return [_rms_norm(jnp.concatenate(outs[k], axis=1), nfg) for k in range(n)]


def _peer_kernel(e_ref, g_ref, xn2_ref, x2_ref, nfg_ref, w_hbm, y_out,
                 e_smem, g_vmem, wbuf, sem_e, sem_w):
    to_smem = pltpu.make_async_copy(e_ref, e_smem, sem_e)
    to_smem.start()
    g_vmem[...] = g_ref[...].T
    to_smem.wait()

    def row_copy(t, slot, j):
        return pltpu.make_async_copy(w_hbm.at[e_smem[j, t]], wbuf.at[slot, :, j, :], sem_w.at[slot])

    def wait_slot(slot):
        pltpu.make_async_copy(wbuf.at[(slot + 1) % PEER_SLOTS], wbuf.at[slot], sem_w.at[slot]).wait()

    def mix_pair(t, slots, t_next, next_slots):
        starts = [(k, j) for k in range(2) for j in range(PEER_SEL)]
        per_step = len(starts) // 32

        def start_some(step):
            for k, j in starts[step * per_step:(step + 1) * per_step]:
                row_copy(t_next + k, next_slots[k], j).start(priority=j % 2)

        rows = lambda ref: [ref[pl.ds(t + k, 1), :] for k in range(2)]
        ys = _mix_tokens(lambda k, s: wbuf[slots[k], s], rows(xn2_ref), rows(g_vmem), rows(x2_ref),
                         nfg_ref[...], start_some)
        for k in range(2):
            y_out[pl.ds(t + k, 1), :] = ys[k]

    for j in range(PEER_SEL):
        row_copy(0, 0, j).start(priority=j % 2)
        row_copy(1, 1, j).start(priority=j % 2)

    def body(i, carry):
        t = PEER_SLOTS * i
        wait_slot(0)
        wait_slot(1)
        mix_pair(t, (0, 1), t + 2, (2, 3))
        wait_slot(2)
        wait_slot(3)
        mix_pair(t + 2, (2, 3), jnp.minimum(t + 4, TOK_TILE - 2), (0, 1))
        return carry

    lax.fori_loop(0, TOK_TILE // PEER_SLOTS, body, 0)
    wait_slot(0)
    wait_slot(1)


def _peer_call(eidx, gate, xn2, x2, nfg, w3, first_tile, n_tiles):
    sel = pl.BlockSpec((PEER_SEL, TOK_TILE), lambda i: (0, i + first_tile))
    tok = pl.BlockSpec((TOK_TILE, D_MODEL), lambda i: (i + first_tile, 0))
    return pl.pallas_call(
        _peer_kernel, grid=(n_tiles,),
        in_specs=[sel, sel, tok, tok, pl.BlockSpec((1, D_MODEL), lambda i: (0, 0)),
                  pl.BlockSpec(memory_space=pl.ANY)],
        out_specs=pl.BlockSpec((TOK_TILE, D_MODEL), lambda i: (i, 0)),
        out_shape=jax.ShapeDtypeStruct((n_tiles * TOK_TILE, D_MODEL), F32),
        scratch_shapes=[pltpu.SMEM((PEER_SEL, TOK_TILE), jnp.int32),
                        pltpu.VMEM((TOK_TILE, PEER_SEL), F32),
                        pltpu.VMEM((PEER_SLOTS, 8, PEER_SEL, 128), jnp.uint32),
                        pltpu.SemaphoreType.DMA,
                        pltpu.SemaphoreType.DMA((PEER_SLOTS,))],
        compiler_params=pltpu.CompilerParams(dimension_semantics=("arbitrary",)),
        name="peer",
    )(eidx, gate, xn2, x2, nfg, w3)


def _sc_gather_call(w3, idx):
    rows = idx.shape[0]
    per_worker = rows // (SC_CORES * SC_SUBCORES)
    mesh = plsc.VectorSubcoreMesh(core_axis_name="c", subcore_axis_name="s",
                                  num_cores=SC_CORES, num_subcores=SC_SUBCORES)

    @functools.partial(
        pl.kernel, mesh=mesh, out_type=jax.ShapeDtypeStruct((rows, 8, 128), jnp.uint32),
        scratch_types=[pltpu.VMEM((SC_WINDOW,), jnp.int32),
                       pltpu.VMEM((SC_WINDOW, 8, 128), jnp.uint32),
                       pltpu.SemaphoreType.DMA],
        name="sc_gather")
    def gather(table_hbm, idx_hbm, out_hbm, idx_v, rows_v, sem):
        base = (lax.axis_index("s") * SC_CORES + lax.axis_index("c")) * per_worker

        @pl.loop(0, per_worker // SC_WINDOW)
        def _(win):
            off = base + win * SC_WINDOW
            pltpu.sync_copy(idx_hbm.at[pl.ds(off, SC_WINDOW)], idx_v)
            pltpu.async_copy(table_hbm.at[idx_v], rows_v, sem).wait()
            pltpu.sync_copy(rows_v, out_hbm.at[pl.ds(off, SC_WINDOW)])

    return gather(w3, idx)


def _peer_staged_kernel(g_ref, xn2_ref, x2_ref, nfg_ref, w_ref, y_out, g_vmem):
    sub = pl.program_id(1)

    @pl.when(sub == 0)
    def _():
        g_vmem[...] = g_ref[...].T

    plane = lambda k, s: w_ref[pl.ds(k * (8 * PEER_SEL) + s, PEER_SEL, stride=8), :]
    rows = lambda ref: [ref[k:k + 1, :] for k in range(STAGE_TOKENS)]
    gates = [g_vmem[pl.ds(sub * STAGE_TOKENS + k, 1), :] for k in range(STAGE_TOKENS)]
    ys = _mix_tokens(plane, rows(xn2_ref), gates, rows(x2_ref), nfg_ref[...], lambda step: None)
    y_out[...] = jnp.concatenate(ys, axis=0)


def _peer_staged_call(gate, xn2, x2, nfg, staged, first_tile, n_tiles):
    subs = TOK_TILE // STAGE_TOKENS
    tok = pl.BlockSpec((STAGE_TOKENS, D_MODEL), lambda i, j: ((i + first_tile) * subs + j, 0))
    staged2d = staged.reshape(-1, 128)
    return pl.pallas_call(
        _peer_staged_kernel, grid=(n_tiles, subs),
        in_specs=[pl.BlockSpec((PEER_SEL, TOK_TILE), lambda i, j: (0, i + first_tile)), tok, tok,
                  pl.BlockSpec((1, D_MODEL), lambda i, j: (0, 0)),
                  pl.BlockSpec((STAGE_TOKENS * PEER_SEL * 8, 128), lambda i, j: (i * subs + j, 0))],
        out_specs=pl.BlockSpec((STAGE_TOKENS, D_MODEL), lambda i, j: (i * subs + j, 0)),
        out_shape=jax.ShapeDtypeStruct((n_tiles * TOK_TILE, D_MODEL), F32),
        scratch_shapes=[pltpu.VMEM((TOK_TILE, PEER_SEL), F32)],
        compiler_params=pltpu.CompilerParams(dimension_semantics=("arbitrary", "arbitrary")),
        name="peer_staged",
    )(gate, xn2, x2, nfg, staged2d)


def _tile_choices(bsz, seq):
    tt = min(seq, 256)
    chunk = min(seq, 64)
    post = min(bsz * seq, 256)
    return tt, chunk, post


def _sc_tiles(tiles):
    return (tiles * SC_SHARE_PERCENT // 100) if tiles >= SC_MIN_TILES else 0


def _mixer_stage(x, st_conv, st_shift, st_wkv, w):
    bsz, seq, _ = x.shape
    n = bsz * seq
    tt, chunk, post_tt = _tile_choices(bsz, seq)
    (r, lw, k, v, a, b, bonus, g, ma, sgb, new_shift, new_conv) = _pre_call(
        x, st_conv, st_shift.reshape(bsz, 1, D_MODEL), w, tt)
    y, new_wkv = _wkv_call(r, lw, k, v, a, b, st_wkv, chunk)
    flat = lambda t: t.reshape(n, t.shape[-1])
    x2, xn2, q = _post_call(flat(y), flat(bonus), flat(g), flat(ma), flat(sgb), flat(x), w, post_tt)
    eidx, gate = _topk_call(q, w["keys_hi"], w["keys_lo"])
    states = (new_conv[None], new_shift.reshape(1, bsz, D_MODEL), new_wkv[None])
    return (x.shape, eidx, gate, xn2, x2), states


def _peer_stage(sel, w):
    (bsz, seq, _), eidx, gate, xn2, x2 = sel
    tiles = bsz * seq // TOK_TILE
    sc_tiles = _sc_tiles(tiles)
    pieces = []
    bounds = [sc_tiles * c // SC_CHUNKS for c in range(SC_CHUNKS + 1)] if sc_tiles else []
    staged = [_sc_gather_call(w["peer_w"], eidx[:, lo * TOK_TILE:hi * TOK_TILE].T.reshape(-1))
              for lo, hi in zip(bounds[:-1], bounds[1:])]
    own = _peer_call(eidx, gate, xn2, x2, w["norm_f_g"], w["peer_w"], sc_tiles, tiles - sc_tiles)
    for rows, lo, hi in zip(staged, bounds[:-1], bounds[1:]):
        pieces.append(_peer_staged_call(gate, xn2, x2, w["norm_f_g"], rows, lo, hi - lo))
    return jnp.concatenate(pieces + [own]).reshape(bsz, seq, D_MODEL)


def kernel(x_prompt, x_sample, state_conv, state_shift, state_wkv, norm1_g, w_in, conv_w, mu_rkv, mu_wag,
           w0, w1, w2, a0, a1, a2, g1, g2, k_k, k_a, r_k, gn_w, gn_b, w_pa, w_pb, w_o, norm2_g,
           peer_wq, peer_keys, peer_u, peer_v, norm_f_g):
    row = lambda t: t.reshape(1, -1)
    head = jnp.arange(D_RWKV) // HEAD_DIM
    wq_hi, wq_lo = _split(peer_wq[0])
    keys = peer_keys[0]
    keys_hi, keys_lo = _split(keys)
    w = dict(
        norm1_g=norm1_g, w_in=w_in[0].astype(BF16), conv_w=conv_w[0], mu_rkv=mu_rkv, mu_wag=mu_wag[0],
        w0=w0, w1=w1[0].astype(BF16), w2=w2[0].astype(BF16), a0=a0, a1=a1[0].astype(BF16),
        a2=a2[0].astype(BF16), g1=g1[0].astype(BF16), g2=g2[0].astype(BF16), k_k=k_k, k_a=k_a,
        r_k=row(r_k[0]), gn_w=gn_w, gn_b=gn_b, w_pa=w_pa[0].astype(BF16), w_pb=w_pb[0].astype(BF16),
        w_o=w_o[0].astype(BF16), norm2_g=norm2_g, wq_hi=wq_hi, wq_lo=wq_lo, keys_hi=keys_hi,
        keys_lo=keys_lo, norm_f_g=row(norm_f_g),
        peer_w=_pack_call(peer_u[0], peer_v[0]).reshape(-1, 8, 128),
        hsum=(head[:, None] == head[None, :]).astype(BF16),
    )
    bp = x_prompt.shape[0]
    per = bp // PROMPT_GROUPS
    zero_conv = jnp.zeros((per, CONV_W - 1, D_CONV), F32)
    zero_shift = jnp.zeros((per, D_MODEL), F32)
    zero_wkv = jnp.zeros((per, HEADS, HEAD_DIM, HEAD_DIM), F32)
    groups = [(x_sample, state_conv[0], state_shift[0], state_wkv[0])]
    groups += [(x_prompt[i * per:(i + 1) * per], zero_conv, zero_shift, zero_wkv) for i in range(PROMPT_GROUPS)]
    mixed = [_mixer_stage(*grp, w) for grp in groups]
    ys = [_peer_stage(sel, w) for sel, _ in mixed]
    (conv_s, shift_s, wkv_s), prompt_states = mixed[0][1], [st for _, st in mixed[1:]]
    conv_p, shift_p, wkv_p = (jnp.concatenate(parts, axis=1) for parts in zip(*prompt_states))
    return (jnp.concatenate(ys[1:], axis=0), ys[0], conv_p, shift_p, wkv_p, conv_s, shift_s, wkv_s)
```

```python
import functools

import jax
import jax.numpy as jnp
from jax import lax
from jax.experimental import pallas as pl
from jax.experimental.pallas import tpu as pltpu
from jax.experimental.pallas import tpu_sc as plsc

F32 = jnp.float32
BF16 = jnp.bfloat16

D_MODEL = 1024
D_CONV = 512
CONV_W = 3
HEADS = 8
HEAD_DIM = 64
D_RWKV = HEADS * HEAD_DIM
PAIR = 2 * HEAD_DIM
N_PAIRS = HEADS // 2
GN_EPS = 64e-5
RMS_EPS = 1e-6
OFF_RKV = 3 * D_CONV
OFF_GATE = OFF_RKV + 3 * D_RWKV
D_IN = OFF_GATE + 2 * D_MODEL

PEER_HEADS = 8
PEER_KEYS = 128
PEER_HALF = 128
PEER_TOPK = 16
PEER_SEL = PEER_HEADS * PEER_TOPK
TOK_TILE = 128
PEER_SLOTS = 4
SC_CORES = 2
SC_SUBCORES = 16
SC_WINDOW = 64
SC_SHARE_PERCENT = 100
SC_CHUNK_TILES = 32
PROMPT_GROUPS = 4
SC_MIN_TILES = 32
STAGE_TOKENS = 8

VMEM_LIMIT_BYTES = 56 * 1024 * 1024


def _dot(a, b):
    return jnp.dot(a.astype(BF16), b.astype(BF16), preferred_element_type=F32)


def _dot_nt(a, b):
    return lax.dot_general(a.astype(BF16), b.astype(BF16), (((1,), (1,)), ((), ())),
                           preferred_element_type=F32)


def _dot_tn(a, b):
    return lax.dot_general(a.astype(BF16), b.astype(BF16), (((0,), (0,)), ((), ())),
                           preferred_element_type=F32)


def _split(a):
    hi = a.astype(BF16)
    lo = (a - hi.astype(F32)).astype(BF16)
    return hi, lo


def _dot_hl(a, w_bf16):
    hi, lo = _split(a)
    return (jnp.dot(hi, w_bf16, preferred_element_type=F32)
            + jnp.dot(lo, w_bf16, preferred_element_type=F32))


def _sigmoid(x):
    return 1.0 / (1.0 + jnp.exp(-x))


def _rms_norm(x, g):
    return x * lax.rsqrt(jnp.mean(x * x, axis=-1, keepdims=True) + RMS_EPS) * g


def _shift_rows(a, carry, n):
    rolled = pltpu.roll(a, n, 0)
    row = lax.broadcasted_iota(jnp.int32, a.shape, 0)
    for i in range(n):
        rolled = jnp.where(row == i, carry[i:i + 1], rolled)
    return rolled


def _pre_kernel(x_ref, conv0_ref, shift0_ref, n1g_ref, w_in_ref, convw_ref, mu_rkv_ref, mu_wag_ref,
                w0_ref, w1_ref, w2_ref, a0_ref, a1_ref, a2_ref, g1_ref, g2_ref, kk_ref, ka_ref, rk_ref,
                w_pa_ref, hsum_ref,
                r_out, lw_out, k_out, v_out, a_out, b_out, bonus_out, g_out, ma_out, sgb_out,
                nshift_out, nconv_out,
                xn_c, zrkv_c, u_c):
    t = pl.program_id(1)
    tt = x_ref.shape[1]
    xn = _rms_norm(x_ref[0], n1g_ref[...])
    xnb = xn.astype(BF16)

    @pl.when(t == 0)
    def _():
        prev = jnp.broadcast_to(shift0_ref[0], (8, D_MODEL))
        xn_c[...] = prev
        zrkv_c[...] = jnp.dot(prev.astype(BF16), w_in_ref[:, OFF_RKV:OFF_GATE],
                              preferred_element_type=F32)
        u_c[0:2, :] = conv0_ref[0]

    zbch = jnp.dot(xnb, w_in_ref[:, 0:OFF_RKV], preferred_element_type=F32)
    zb = zbch[:, 0:D_CONV]
    u = zbch[:, D_CONV:2 * D_CONV] * zbch[:, 2 * D_CONV:3 * D_CONV]
    u_prev = u_c[0:2, :]
    u1 = _shift_rows(u, u_prev[1:2], 1)
    u2 = _shift_rows(u, u_prev, 2)
    cw = convw_ref[...]
    y_a = zb * (cw[0:1] * u2 + cw[1:2] * u1 + cw[2:3] * u)
    u_last = u[tt - 2:tt, :]
    nconv_out[0] = u_last
    u_c[0:2, :] = u_last

    zg = jnp.dot(xnb, w_in_ref[:, OFF_GATE:D_IN], preferred_element_type=F32)
    ma_out[0] = _sigmoid(zg[:, 0:D_MODEL]) * _dot(y_a, w_pa_ref[...])
    sgb_out[0] = _sigmoid(zg[:, D_MODEL:2 * D_MODEL])

    zrkv = jnp.dot(xnb, w_in_ref[:, OFF_RKV:OFF_GATE], preferred_element_type=F32)
    zprev = _shift_rows(zrkv, zrkv_c[0:1, :], 1)
    zs = zrkv + mu_rkv_ref[...] * (zprev - zrkv)
    xprev = _shift_rows(xn, xn_c[0:1, :], 1)
    dx = xprev - xn
    mu = mu_wag_ref[...]
    xw = xn + dx * mu[0:1]
    xa = xn + dx * mu[1:2]
    xg = xn + dx * mu[2:3]
    xn_last = xn[tt - 1:tt, :]
    nshift_out[0] = xn_last
    xn_c[0:1, :] = xn_last
    zrkv_c[0:1, :] = zrkv[tt - 1:tt, :]

    wl = w0_ref[...] + _dot(jnp.tanh(_dot(xw, w1_ref[...])), w2_ref[...])
    softplus = jnp.maximum(-wl, 0.0) + jnp.log(1.0 + jnp.exp(-jnp.abs(wl)))
    lw_out[0] = -jnp.exp(-softplus - 0.5)
    a_sig = _sigmoid(a0_ref[...] + _dot(_dot(xa, a1_ref[...]), a2_ref[...]))
    g_out[0] = _dot(_sigmoid(_dot(xg, g1_ref[...])), g2_ref[...])

    r = zs[:, 0:D_RWKV]
    k = zs[:, D_RWKV:2 * D_RWKV]
    v = zs[:, 2 * D_RWKV:3 * D_RWKV]
    hsum = hsum_ref[...]
    kk = k * kk_ref[...]
    kk = kk / jnp.maximum(jnp.sqrt(_dot_hl(kk * kk, hsum)), 1e-12)
    k = k * (1.0 + (a_sig - 1.0) * ka_ref[...])
    r_out[0] = r
    k_out[0] = k
    v_out[0] = v
    a_out[0] = -kk
    b_out[0] = kk * a_sig
    bonus_out[0] = _dot_hl(r * k * rk_ref[...], hsum) * v


def _pre_call(x, conv0, shift0, w, tt):
    bsz, seq, _ = x.shape
    grid = (bsz, seq // tt)
    row = lambda b, t: (b, t, 0)
    per_b = lambda b, t: (b, 0, 0)
    const2 = lambda b, t: (0, 0)

    def tok(c):
        return pl.BlockSpec((1, tt, c), row)

    def full(a):
        return pl.BlockSpec(a.shape, const2)

    weights = (w["norm1_g"], w["w_in"], w["conv_w"], w["mu_rkv"], w["mu_wag"], w["w0"], w["w1"], w["w2"],
               w["a0"], w["a1"], w["a2"], w["g1"], w["g2"], w["k_k"], w["k_a"], w["r_k"], w["w_pa"],
               w["hsum"])
    in_specs = [tok(D_MODEL), pl.BlockSpec((1, CONV_W - 1, D_CONV), per_b),
                pl.BlockSpec((1, 1, D_MODEL), per_b)] + [full(a) for a in weights]
    tok_shape = lambda c: jax.ShapeDtypeStruct((bsz, seq, c), F32)
    out_shape = [tok_shape(D_RWKV)] * 8 + [tok_shape(D_MODEL)] * 2 + [
        jax.ShapeDtypeStruct((bsz, 1, D_MODEL), F32),
        jax.ShapeDtypeStruct((bsz, CONV_W - 1, D_CONV), F32)]
    out_specs = [tok(D_RWKV)] * 8 + [tok(D_MODEL)] * 2 + [
        pl.BlockSpec((1, 1, D_MODEL), per_b), pl.BlockSpec((1, CONV_W - 1, D_CONV), per_b)]
    return pl.pallas_call(
        _pre_kernel, grid=grid, in_specs=in_specs, out_specs=out_specs, out_shape=out_shape,
        scratch_shapes=[pltpu.VMEM((8, D_MODEL), F32), pltpu.VMEM((8, 3 * D_RWKV), F32),
                        pltpu.VMEM((8, D_CONV), F32)],
        compiler_params=pltpu.CompilerParams(dimension_semantics=("arbitrary", "arbitrary"),
                                             vmem_limit_bytes=VMEM_LIMIT_BYTES),
        name="pre",
    )(x, conv0, shift0, *weights)


def _wkv_kernel(r_ref, lw_ref, k_ref, v_ref, a_ref, b_ref, s0_ref, tri_ref,
                y_out, s_out, s_c):
    c = pl.program_id(1)
    L = r_ref.shape[1]
    pairs = range(N_PAIRS)
    lane = lax.broadcasted_iota(jnp.int32, (L, PAIR), 1)
    first = lane < HEAD_DIM
    s_row = lax.broadcasted_iota(jnp.int32, (PAIR, PAIR), 0)
    s_col = lax.broadcasted_iota(jnp.int32, (PAIR, PAIR), 1)
    s_mask = (s_row < HEAD_DIM) == (s_col < HEAD_DIM)

    @pl.when(c == 0)
    def _():
        z = jnp.zeros((HEAD_DIM, HEAD_DIM), F32)
        for p in pairs:
            s_c[p] = jnp.concatenate([jnp.concatenate([s0_ref[0, 2 * p], z], axis=1),
                                      jnp.concatenate([z, s0_ref[0, 2 * p + 1]], axis=1)], axis=0)

    def load(ref):
        return [ref[0, :, p * PAIR:(p + 1) * PAIR] for p in pairs]

    S = [s_c[p] for p in pairs]
    r, lw, k, v, a, b = (load(ref) for ref in (r_ref, lw_ref, k_ref, v_ref, a_ref, b_ref))

    tri = tri_ref[...]

    def cumsum_rows(x):
        l1 = x.astype(BF16)
        r1 = x - l1.astype(F32)
        l2 = r1.astype(BF16)
        l3 = (r1 - l2.astype(F32)).astype(BF16)
        return (jnp.dot(tri, l1, preferred_element_type=F32) + jnp.dot(tri, l2, preferred_element_type=F32)
                + jnp.dot(tri, l3, preferred_element_type=F32))

    cum = [cumsum_rows(x) for x in lw]
    cum_l = [x[L - 1:L, :] for x in cum]
    w_inv = [jnp.exp(-x) for x in cum]
    at = [a[p] * jnp.exp(cum[p] - lw[p]) for p in pairs]
    bt = [b[p] * w_inv[p] for p in pairs]
    kt = [k[p] * w_inv[p] for p in pairs]
    rt = [r[p] * jnp.exp(cum[p]) for p in pairs]
    dec = [jnp.exp(cum_l[p] - cum[p]) for p in pairs]

    row = lax.broadcasted_iota(jnp.int32, (L, L), 0)
    col = lax.broadcasted_iota(jnp.int32, (L, L), 1)
    strict = row > col
    incl = row >= col
    zero = jnp.zeros((L, L), F32)

    def per_head(x):
        return jnp.where(first, x, 0.0), jnp.where(first, 0.0, x)

    def merge(x1, x2):
        return jnp.where(first, x1, x2)

    def both(ms, x):
        return merge(_dot(ms[0], x), _dot(ms[1], x))

    at_h = [per_head(x) for x in at]
    rt_h = [per_head(x) for x in rt]
    mab = [[jnp.where(strict, _dot_nt(x, bt[p]), zero) for x in at_h[p]] for p in pairs]
    mak = [[jnp.where(strict, _dot_nt(x, kt[p]), zero) for x in at_h[p]] for p in pairs]
    nrb = [[jnp.where(incl, _dot_nt(x, bt[p]), zero) for x in rt_h[p]] for p in pairs]
    nrk = [[jnp.where(incl, _dot_nt(x, kt[p]), zero) for x in rt_h[p]] for p in pairs]

    U = [_dot_nt(at[p], S[p]) + both(mak[p], v[p]) for p in pairs]
    n = 1
    while n < L:
        U = [U[p] + both(mab[p], U[p]) for p in pairs]
        n *= 2
        if n < L:
            mab = [[_dot(m, m) for m in mab[p]] for p in pairs]
    for p in pairs:
        y_out[0, :, p * PAIR:(p + 1) * PAIR] = (_dot_nt(rt[p], S[p]) + both(nrb[p], U[p])
                                                + both(nrk[p], v[p]))
        s_new = S[p] * jnp.exp(cum_l[p]) + jnp.where(
            s_mask, _dot_tn(U[p], b[p] * dec[p]) + _dot_tn(v[p], k[p] * dec[p]), 0.0)
        s_c[p] = s_new
        s_out[0, 2 * p] = s_new[0:HEAD_DIM, 0:HEAD_DIM]
        s_out[0, 2 * p + 1] = s_new[HEAD_DIM:PAIR, HEAD_DIM:PAIR]


def _wkv_call(r, lw, k, v, a, b, s0, chunk):
    bsz, seq, _ = r.shape
    tok = pl.BlockSpec((1, chunk, D_RWKV), lambda bi, c: (bi, c, 0))
    st = pl.BlockSpec((1, HEADS, HEAD_DIM, HEAD_DIM), lambda bi, c: (bi, 0, 0, 0))
    tri = (jnp.arange(chunk)[:, None] >= jnp.arange(chunk)[None, :]).astype(BF16)
    return pl.pallas_call(
        _wkv_kernel, grid=(bsz, seq // chunk),
        in_specs=[tok] * 6 + [st, pl.BlockSpec((chunk, chunk), lambda bi, c: (0, 0))],
        out_specs=[tok, st],
        out_shape=[jax.ShapeDtypeStruct((bsz, seq, D_RWKV), F32),
                   jax.ShapeDtypeStruct((bsz, HEADS, HEAD_DIM, HEAD_DIM), F32)],
        scratch_shapes=[pltpu.VMEM((N_PAIRS, PAIR, PAIR), F32)],
        compiler_params=pltpu.CompilerParams(dimension_semantics=("arbitrary", "arbitrary")),
        name="wkv",
    )(r, lw, k, v, a, b, s0, tri)


def _post_kernel(y_ref, bonus_ref, g_ref, ma_ref, sgb_ref, x_ref, gnw_ref, gnb_ref, hsum_ref,
                 w_pb_ref, w_o_ref, n2g_ref, wq_hi_ref, wq_lo_ref,
                 x2_out, xn2_out, q_out):
    y = y_ref[...]
    hsum = hsum_ref[...]
    mean = _dot_hl(y, hsum) * (1.0 / HEAD_DIM)
    d = y - mean
    var = _dot_hl(d * d, hsum) * (1.0 / HEAD_DIM)
    yn = d * lax.rsqrt(var + GN_EPS) * gnw_ref[...] + gnb_ref[...] + bonus_ref[...]
    y_b = yn * g_ref[...]
    merged = ma_ref[...] + sgb_ref[...] * _dot(y_b, w_pb_ref[...])
    x2 = x_ref[...] + _dot(merged, w_o_ref[...])
    x2_out[...] = x2
    xn2 = _rms_norm(x2, n2g_ref[...])
    xn2_out[...] = xn2
    hi, lo = _split(xn2)
    wq_hi = wq_hi_ref[...]
    q_out[...] = (jnp.dot(hi, wq_hi, preferred_element_type=F32)
                  + jnp.dot(lo, wq_hi, preferred_element_type=F32)
                  + jnp.dot(hi, wq_lo_ref[...], preferred_element_type=F32))


def _post_call(y, bonus, g, ma, sgb, x, w, tt):
    n = y.shape[0]
    row = lambda i: (i, 0)
    const = lambda i: (0, 0)
    tok = lambda c: pl.BlockSpec((tt, c), row)
    weights = (w["gn_w"], w["gn_b"], w["hsum"], w["w_pb"], w["w_o"], w["norm2_g"], w["wq_hi"], w["wq_lo"])
    d_q = w["wq_hi"].shape[1]
    return pl.pallas_call(
        _post_kernel, grid=(n // tt,),
        in_specs=[tok(D_RWKV)] * 3 + [tok(D_MODEL)] * 3 + [pl.BlockSpec(a.shape, const) for a in weights],
        out_specs=[tok(D_MODEL), tok(D_MODEL), tok(d_q)],
        out_shape=[jax.ShapeDtypeStruct((n, D_MODEL), F32)] * 2 + [jax.ShapeDtypeStruct((n, d_q), F32)],
        compiler_params=pltpu.CompilerParams(dimension_semantics=("arbitrary",),
                                             vmem_limit_bytes=VMEM_LIMIT_BYTES),
        name="post",
    )(y, bonus, g, ma, sgb, x, *weights)


TOPK_HEADS = 4
STAIR_COUNTS = tuple(PEER_TOPK // (a + 1) for a in range(8))
STAIR_ROWS = 16 + 8 * 7 + 8


def _extract_max(s, iota, n_rows):
    m = jnp.max(s, axis=0, keepdims=True)
    idx = jnp.min(jnp.where(s == m, iota, n_rows), axis=0, keepdims=True)
    return m, idx, iota == idx


def _topk_kernel(q_ref, khi_ref, klo_ref, e_out, g_out, s_scr, v_scr, i_scr, c_scr, ci_scr, sc_scr):
    nt = lambda x, y: lax.dot_general(x, y, (((1,), (1,)), ((), ())), preferred_element_type=F32)
    for c in range(2 * TOPK_HEADS):
        h, p = divmod(c, 2)
        q_hi, q_lo = _split(q_ref[:, c * PEER_HALF:(c + 1) * PEER_HALF])
        k_hi = khi_ref[h, p]
        s_scr[c] = nt(k_hi, q_hi) + nt(k_hi, q_lo) + nt(klo_ref[h, p], q_hi)

    iota = lax.broadcasted_iota(jnp.int32, (PEER_KEYS, TOK_TILE), 0)

    def sub_key_step(j, carry):
        for c in range(2 * TOPK_HEADS):
            s = s_scr[c]
            m, idx, hit = _extract_max(s, iota, PEER_KEYS)
            v_scr[c, pl.ds(j, 1), :] = m
            i_scr[c, pl.ds(j, 1), :] = idx
            s_scr[c] = jnp.where(hit, -jnp.inf, s)
        return carry

    lax.fori_loop(0, PEER_TOPK, sub_key_step, 0)

    row8 = lax.broadcasted_iota(jnp.int32, (8, TOK_TILE), 0)
    for h in range(TOPK_HEADS):
        v1, i1 = v_scr[2 * h], i_scr[2 * h] * PEER_KEYS
        v2, i2 = v_scr[2 * h + 1], i_scr[2 * h + 1]
        vals = [v1[0:1] + v2]
        idxs = [i1[0:1] + i2]
        for a in range(1, 8):
            vals.append(jnp.where(row8 < STAIR_COUNTS[a], v1[a:a + 1] + v2[0:8], -jnp.inf))
            idxs.append(i1[a:a + 1] + i2[0:8])
        vals.append(v1[8:16] + v2[0:1])
        idxs.append(i1[8:16] + i2[0:1])
        c_scr[h] = jnp.concatenate(vals, axis=0)
        ci_scr[h] = jnp.concatenate(idxs, axis=0)

    iota_c = lax.broadcasted_iota(jnp.int32, (STAIR_ROWS, TOK_TILE), 0)

    def expert_step(j, carry):
        for h in range(TOPK_HEADS):
            s = c_scr[h]
            m, _, hit = _extract_max(s, iota_c, STAIR_ROWS)
            sc_scr[h, pl.ds(j, 1), :] = m
            e_out[pl.ds(h * PEER_TOPK + j, 1), :] = jnp.max(jnp.where(hit, ci_scr[h], -1), axis=0,
                                                             keepdims=True)
            c_scr[h] = jnp.where(hit, -jnp.inf, s)
        return carry

    lax.fori_loop(0, PEER_TOPK, expert_step, 0)

    for h in range(TOPK_HEADS):
        sc = sc_scr[h]
        e = jnp.exp(sc - sc[0:1])
        g_out[h * PEER_TOPK:(h + 1) * PEER_TOPK, :] = e / jnp.sum(e, axis=0, keepdims=True)


def _topk_call(q, khi, klo):
    n = q.shape[0]
    rows = TOPK_HEADS * PEER_TOPK
    sel = pl.BlockSpec((rows, TOK_TILE), lambda i, h: (h, i))
    keys = pl.BlockSpec((TOPK_HEADS, 2, PEER_KEYS, PEER_HALF), lambda i, h: (h, 0, 0, 0))
    chains = 2 * TOPK_HEADS
    return pl.pallas_call(
        _topk_kernel, grid=(n // TOK_TILE, PEER_HEADS // TOPK_HEADS),
        in_specs=[pl.BlockSpec((TOK_TILE, chains * PEER_HALF), lambda i, h: (i, h)), keys, keys],
        out_specs=[sel, sel],
        out_shape=[jax.ShapeDtypeStruct((PEER_SEL, n), jnp.int32), jax.ShapeDtypeStruct((PEER_SEL, n), F32)],
        scratch_shapes=[pltpu.VMEM((chains, PEER_KEYS, TOK_TILE), F32),
                        pltpu.VMEM((chains, PEER_TOPK, TOK_TILE), F32),
                        pltpu.VMEM((chains, PEER_TOPK, TOK_TILE), jnp.int32),
                        pltpu.VMEM((TOPK_HEADS, STAIR_ROWS, TOK_TILE), F32),
                        pltpu.VMEM((TOPK_HEADS, STAIR_ROWS, TOK_TILE), jnp.int32),
                        pltpu.VMEM((TOPK_HEADS, PEER_TOPK, TOK_TILE), F32)],
        compiler_params=pltpu.CompilerParams(dimension_semantics=("arbitrary", "arbitrary")),
        name="topk",
    )(q, khi, klo)


def _pack_kernel(u_ref, v_ref, w_out):
    ub = lax.bitcast_convert_type(u_ref[...].astype(BF16).astype(F32), jnp.uint32)
    vb = lax.bitcast_convert_type(v_ref[...].astype(BF16).astype(F32), jnp.uint32)
    w_out[...] = (ub & jnp.uint32(0xFFFF0000)) | (vb >> 16)


def _pack_call(u, v):
    n, d = u.shape
    rows = 512
    blk = pl.BlockSpec((rows, d), lambda i: (i, 0))
    return pl.pallas_call(
        _pack_kernel, grid=(n // rows,), in_specs=[blk, blk], out_specs=blk,
        out_shape=jax.ShapeDtypeStruct((n, d), jnp.uint32),
        compiler_params=pltpu.CompilerParams(dimension_semantics=("arbitrary",)),
        name="pack",
    )(u, v)


def _mix_tokens(plane, xn2_rows, gate_rows, x2_rows, nfg, between):
    n = len(xn2_rows)
    xs = [jnp.broadcast_to(xn2_rows[k], (8, D_MODEL)).astype(BF16) for k in range(n)]
    acts = [jnp.zeros((8, PEER_SEL), F32) for _ in range(n)]
    step = 0
    for s in range(8):
        for k in range(n):
            u = lax.bitcast_convert_type(plane(k, s) & jnp.uint32(0xFFFF0000), F32).astype(BF16)
            acts[k] = acts[k] + lax.dot_general(xs[k][:, s * 128:(s + 1) * 128], u,
                                                (((1,), (1,)), ((), ())), preferred_element_type=F32)
            between(step)
            step += 1
    coefs = []
    for k in range(n):
        act = acts[k]
        gelu = 0.5 * act * (1.0 + jnp.tanh(0.7978845608028654 * (act + 0.044715 * (act * act * act))))
        coefs.append((gate_rows[k] * gelu).astype(BF16))
    outs = [[] for _ in range(n)]
    for s in range(8):
        for k in range(n):
            v = lax.bitcast_convert_type(plane(k, s) << 16, F32).astype(BF16)
            mix = jnp.dot(coefs[k], v, preferred_element_type=F32)
            outs[k].append(x2_rows[k][:, s * 128:(s + 1) * 128] + mix[0:1])
            between(step)
            step += 1
    return [_rms_norm(jnp.concatenate(outs[k], axis=1), nfg) for k in range(n)]


def _peer_kernel(e_ref, g_ref, xn2_ref, x2_ref, nfg_ref, w_hbm, y_out,
                 e_smem, g_vmem, wbuf, sem_e, sem_w):
    to_smem = pltpu.make_async_copy(e_ref, e_smem, sem_e)
    to_smem.start()
    g_vmem[...] = g_ref[...].T
    to_smem.wait()

    def row_copy(t, slot, j):
        return pltpu.make_async_copy(w_hbm.at[e_smem[j, t]], wbuf.at[slot, :, j, :], sem_w.at[slot])

    def wait_slot(slot):
        pltpu.make_async_copy(wbuf.at[(slot + 1) % PEER_SLOTS], wbuf.at[slot], sem_w.at[slot]).wait()

    def mix_pair(t, slots, t_next, next_slots):
        starts = [(k, j) for k in range(2) for j in range(PEER_SEL)]
        per_step = len(starts) // 32

        def start_some(step):
            for k, j in starts[step * per_step:(step + 1) * per_step]:
                row_copy(t_next + k, next_slots[k], j).start(priority=j % 2)

        rows = lambda ref: [ref[pl.ds(t + k, 1), :] for k in range(2)]
        ys = _mix_tokens(lambda k, s: wbuf[slots[k], s], rows(xn2_ref), rows(g_vmem), rows(x2_ref),
                         nfg_ref[...], start_some)
        for k in range(2):
            y_out[pl.ds(t + k, 1), :] = ys[k]

    for j in range(PEER_SEL):
        row_copy(0, 0, j).start(priority=j % 2)
        row_copy(1, 1, j).start(priority=j % 2)

    def body(i, carry):
        t = PEER_SLOTS * i
        wait_slot(0)
        wait_slot(1)
        mix_pair(t, (0, 1), t + 2, (2, 3))
        wait_slot(2)
        wait_slot(3)
        mix_pair(t + 2, (2, 3), jnp.minimum(t + 4, TOK_TILE - 2), (0, 1))
        return carry

    lax.fori_loop(0, TOK_TILE // PEER_SLOTS, body, 0)
    wait_slot(0)
    wait_slot(1)


def _peer_call(eidx, gate, xn2, x2, nfg, w3, first_tile, n_tiles):
    sel = pl.BlockSpec((PEER_SEL, TOK_TILE), lambda i: (0, i + first_tile))
    tok = pl.BlockSpec((TOK_TILE, D_MODEL), lambda i: (i + first_tile, 0))
    return pl.pallas_call(
        _peer_kernel, grid=(n_tiles,),
        in_specs=[sel, sel, tok, tok, pl.BlockSpec((1, D_MODEL), lambda i: (0, 0)),
                  pl.BlockSpec(memory_space=pl.ANY)],
        out_specs=pl.BlockSpec((TOK_TILE, D_MODEL), lambda i: (i, 0)),
        out_shape=jax.ShapeDtypeStruct((n_tiles * TOK_TILE, D_MODEL), F32),
        scratch_shapes=[pltpu.SMEM((PEER_SEL, TOK_TILE), jnp.int32),
                        pltpu.VMEM((TOK_TILE, PEER_SEL), F32),
                        pltpu.VMEM((PEER_SLOTS, 8, PEER_SEL, 128), jnp.uint32),
                        pltpu.SemaphoreType.DMA,
                        pltpu.SemaphoreType.DMA((PEER_SLOTS,))],
        compiler_params=pltpu.CompilerParams(dimension_semantics=("arbitrary",)),
        name="peer",
    )(eidx, gate, xn2, x2, nfg, w3)


def _sc_gather_call(w3, idx):
    rows = idx.shape[0]
    per_worker = rows // (SC_CORES * SC_SUBCORES)
    mesh = plsc.VectorSubcoreMesh(core_axis_name="c", subcore_axis_name="s",
                                  num_cores=SC_CORES, num_subcores=SC_SUBCORES)

    @functools.partial(
        pl.kernel, mesh=mesh, out_type=jax.ShapeDtypeStruct((rows, 8, 128), jnp.uint32),
        scratch_types=[pltpu.VMEM((SC_WINDOW,), jnp.int32),
                       pltpu.VMEM((SC_WINDOW, 8, 128), jnp.uint32),
                       pltpu.SemaphoreType.DMA],
        name="sc_gather")
    def gather(table_hbm, idx_hbm, out_hbm, idx_v, rows_v, sem):
        base = (lax.axis_index("s") * SC_CORES + lax.axis_index("c")) * per_worker

        @pl.loop(0, per_worker // SC_WINDOW)
        def _(win):
            off = base + win * SC_WINDOW
            pltpu.sync_copy(idx_hbm.at[pl.ds(off, SC_WINDOW)], idx_v)
            pltpu.async_copy(table_hbm.at[idx_v], rows_v, sem).wait()
            pltpu.sync_copy(rows_v, out_hbm.at[pl.ds(off, SC_WINDOW)])

    return gather(w3, idx)


def _peer_staged_kernel(g_ref, xn2_ref, x2_ref, nfg_ref, w_ref, y_out, g_vmem):
    sub = pl.program_id(1)

    @pl.when(sub == 0)
    def _():
        g_vmem[...] = g_ref[...].T

    plane = lambda k, s: w_ref[pl.ds(k * (8 * PEER_SEL) + s, PEER_SEL, stride=8), :]
    rows = lambda ref: [ref[k:k + 1, :] for k in range(STAGE_TOKENS)]
    gates = [g_vmem[pl.ds(sub * STAGE_TOKENS + k, 1), :] for k in range(STAGE_TOKENS)]
    ys = _mix_tokens(plane, rows(xn2_ref), gates, rows(x2_ref), nfg_ref[...], lambda step: None)
    y_out[...] = jnp.concatenate(ys, axis=0)


def _peer_staged_call(gate, xn2, x2, nfg, staged, first_tile, n_tiles):
    subs = TOK_TILE // STAGE_TOKENS
    tok = pl.BlockSpec((STAGE_TOKENS, D_MODEL), lambda i, j: ((i + first_tile) * subs + j, 0))
    staged2d = staged.reshape(-1, 128)
    return pl.pallas_call(
        _peer_staged_kernel, grid=(n_tiles, subs),
        in_specs=[pl.BlockSpec((PEER_SEL, TOK_TILE), lambda i, j: (0, i + first_tile)), tok, tok,
                  pl.BlockSpec((1, D_MODEL), lambda i, j: (0, 0)),
                  pl.BlockSpec((STAGE_TOKENS * PEER_SEL * 8, 128), lambda i, j: (i * subs + j, 0))],
        out_specs=pl.BlockSpec((STAGE_TOKENS, D_MODEL), lambda i, j: (i * subs + j, 0)),
        out_shape=jax.ShapeDtypeStruct((n_tiles * TOK_TILE, D_MODEL), F32),
        scratch_shapes=[pltpu.VMEM((TOK_TILE, PEER_SEL), F32)],
        compiler_params=pltpu.CompilerParams(dimension_semantics=("arbitrary", "arbitrary")),
        name="peer_staged",
    )(gate, xn2, x2, nfg, staged2d)


def _tile_choices(bsz, seq):
    tt = min(seq, 256)
    chunk = min(seq, 64)
    post = min(bsz * seq, 256)
    return tt, chunk, post


def _sc_tiles(tiles):
    return (tiles * SC_SHARE_PERCENT // 100) if tiles >= SC_MIN_TILES else 0


def _mixer_stage(x, st_conv, st_shift, st_wkv, w):
    bsz, seq, _ = x.shape
    n = bsz * seq
    tt, chunk, post_tt = _tile_choices(bsz, seq)
    (r, lw, k, v, a, b, bonus, g, ma, sgb, new_shift, new_conv) = _pre_call(
        x, st_conv, st_shift.reshape(bsz, 1, D_MODEL), w, tt)
    y, new_wkv = _wkv_call(r, lw, k, v, a, b, st_wkv, chunk)
    flat = lambda t: t.reshape(n, t.shape[-1])
    x2, xn2, q = _post_call(flat(y), flat(bonus), flat(g), flat(ma), flat(sgb), flat(x), w, post_tt)
    eidx, gate = _topk_call(q, w["keys_hi"], w["keys_lo"])
    states = (new_conv[None], new_shift.reshape(1, bsz, D_MODEL), new_wkv[None])
    return (x.shape, eidx, gate, xn2, x2), states


def _peer_stage(sel, w):
    (bsz, seq, _), eidx, gate, xn2, x2 = sel
    tiles = bsz * seq // TOK_TILE
    sc_tiles = _sc_tiles(tiles)
    pieces = []
    bounds = list(range(0, sc_tiles, SC_CHUNK_TILES)) + [sc_tiles] if sc_tiles else []
    staged = [_sc_gather_call(w["peer_w"], eidx[:, lo * TOK_TILE:hi * TOK_TILE].T.reshape(-1))
              for lo, hi in zip(bounds[:-1], bounds[1:])]
    if tiles > sc_tiles:
        own = _peer_call(eidx, gate, xn2, x2, w["norm_f_g"], w["peer_w"], sc_tiles, tiles - sc_tiles)
    for rows, lo, hi in zip(staged, bounds[:-1], bounds[1:]):
        pieces.append(_peer_staged_call(gate, xn2, x2, w["norm_f_g"], rows, lo, hi - lo))
    if tiles > sc_tiles:
        pieces.append(own)
    return jnp.concatenate(pieces).reshape(bsz, seq, D_MODEL)


def kernel(x_prompt, x_sample, state_conv, state_shift, state_wkv, norm1_g, w_in, conv_w, mu_rkv, mu_wag,
           w0, w1, w2, a0, a1, a2, g1, g2, k_k, k_a, r_k, gn_w, gn_b, w_pa, w_pb, w_o, norm2_g,
           peer_wq, peer_keys, peer_u, peer_v, norm_f_g):
    row = lambda t: t.reshape(1, -1)
    head = jnp.arange(D_RWKV) // HEAD_DIM
    wq_hi, wq_lo = _split(peer_wq[0])
    keys = peer_keys[0]
    keys_hi, keys_lo = _split(keys)
    w = dict(
        norm1_g=norm1_g, w_in=w_in[0].astype(BF16), conv_w=conv_w[0], mu_rkv=mu_rkv, mu_wag=mu_wag[0],
        w0=w0, w1=w1[0].astype(BF16), w2=w2[0].astype(BF16), a0=a0, a1=a1[0].astype(BF16),
        a2=a2[0].astype(BF16), g1=g1[0].astype(BF16), g2=g2[0].astype(BF16), k_k=k_k, k_a=k_a,
        r_k=row(r_k[0]), gn_w=gn_w, gn_b=gn_b, w_pa=w_pa[0].astype(BF16), w_pb=w_pb[0].astype(BF16),
        w_o=w_o[0].astype(BF16), norm2_g=norm2_g, wq_hi=wq_hi, wq_lo=wq_lo, keys_hi=keys_hi,
        keys_lo=keys_lo, norm_f_g=row(norm_f_g),
        peer_w=_pack_call(peer_u[0], peer_v[0]).reshape(-1, 8, 128),
        hsum=(head[:, None] == head[None, :]).astype(BF16),
    )
    bp = x_prompt.shape[0]
    per = bp // PROMPT_GROUPS
    zero_conv = jnp.zeros((per, CONV_W - 1, D_CONV), F32)
    zero_shift = jnp.zeros((per, D_MODEL), F32)
    zero_wkv = jnp.zeros((per, HEADS, HEAD_DIM, HEAD_DIM), F32)
    groups = [(x_sample, state_conv[0], state_shift[0], state_wkv[0])]
    groups += [(x_prompt[i * per:(i + 1) * per], zero_conv, zero_shift, zero_wkv) for i in range(PROMPT_GROUPS)]
    mixed = [_mixer_stage(*grp, w) for grp in groups]
    ys = [_peer_stage(sel, w) for sel, _ in mixed]
    (conv_s, shift_s, wkv_s), prompt_states = mixed[0][1], [st for _, st in mixed[1:]]
    conv_p, shift_p, wkv_p = (jnp.concatenate(parts, axis=1) for parts in zip(*prompt_states))
    return (jnp.concatenate(ys[1:], axis=0), ys[0], conv_p, shift_p, wkv_p, conv_s, shift_s, wkv_s)
```

```python
import functools

import jax
import jax.numpy as jnp
from jax import lax
from jax.experimental import pallas as pl
from jax.experimental.pallas import tpu as pltpu
from jax.experimental.pallas import tpu_sc as plsc

F32 = jnp.float32
BF16 = jnp.bfloat16

D_MODEL = 1024
D_CONV = 512
CONV_W = 3
HEADS = 8
HEAD_DIM = 64
D_RWKV = HEADS * HEAD_DIM
PAIR = 2 * HEAD_DIM
N_PAIRS = HEADS // 2
GN_EPS = 64e-5
RMS_EPS = 1e-6
OFF_RKV = 3 * D_CONV
OFF_GATE = OFF_RKV + 3 * D_RWKV
D_IN = OFF_GATE + 2 * D_MODEL

PEER_HEADS = 8
PEER_KEYS = 128
PEER_HALF = 128
PEER_TOPK = 16
PEER_SEL = PEER_HEADS * PEER_TOPK
TOK_TILE = 128
PEER_SLOTS = 4
SC_CORES = 2
SC_SUBCORES = 16
SC_WINDOW = 64
SC_CHUNK_TILES = 32
PROMPT_GROUPS = 8
SC_MIN_TILES = 32
STAGE_TOKENS = 16

VMEM_LIMIT_BYTES = 56 * 1024 * 1024


def _dot(a, b):
    return jnp.dot(a.astype(BF16), b.astype(BF16), preferred_element_type=F32)


def _dot_nt(a, b):
    return lax.dot_general(a.astype(BF16), b.astype(BF16), (((1,), (1,)), ((), ())),
                           preferred_element_type=F32)


def _dot_tn(a, b):
    return lax.dot_general(a.astype(BF16), b.astype(BF16), (((0,), (0,)), ((), ())),
                           preferred_element_type=F32)


def _split(a):
    hi = a.astype(BF16)
    lo = (a - hi.astype(F32)).astype(BF16)
    return hi, lo


def _dot_hl(a, w_bf16):
    hi, lo = _split(a)
    return (jnp.dot(hi, w_bf16, preferred_element_type=F32)
            + jnp.dot(lo, w_bf16, preferred_element_type=F32))


def _sigmoid(x):
    return 1.0 / (1.0 + jnp.exp(-x))


def _rms_norm(x, g):
    return x * lax.rsqrt(jnp.mean(x * x, axis=-1, keepdims=True) + RMS_EPS) * g


def _shift_rows(a, carry, n):
    rolled = pltpu.roll(a, n, 0)
    row = lax.broadcasted_iota(jnp.int32, a.shape, 0)
    for i in range(n):
        rolled = jnp.where(row == i, carry[i:i + 1], rolled)
    return rolled


def _pre_kernel(x_ref, conv0_ref, shift0_ref, n1g_ref, w_in_ref, convw_ref, mu_rkv_ref, mu_wag_ref,
                w0_ref, w1_ref, w2_ref, a0_ref, a1_ref, a2_ref, g1_ref, g2_ref, kk_ref, ka_ref, rk_ref,
                w_pa_ref, hsum_ref,
                r_out, lw_out, k_out, v_out, a_out, b_out, bonus_out, g_out, ma_out, sgb_out,
                nshift_out, nconv_out,
                xn_c, zrkv_c, u_c):
    t = pl.program_id(1)
    tt = x_ref.shape[1]
    xn = _rms_norm(x_ref[0], n1g_ref[...])
    xnb = xn.astype(BF16)

    @pl.when(t == 0)
    def _():
        prev = jnp.broadcast_to(shift0_ref[0], (8, D_MODEL))
        xn_c[...] = prev
        zrkv_c[...] = jnp.dot(prev.astype(BF16), w_in_ref[:, OFF_RKV:OFF_GATE],
                              preferred_element_type=F32)
        u_c[0:2, :] = conv0_ref[0]

    zbch = jnp.dot(xnb, w_in_ref[:, 0:OFF_RKV], preferred_element_type=F32)
    zb = zbch[:, 0:D_CONV]
    u = zbch[:, D_CONV:2 * D_CONV] * zbch[:, 2 * D_CONV:3 * D_CONV]
    u_prev = u_c[0:2, :]
    u1 = _shift_rows(u, u_prev[1:2], 1)
    u2 = _shift_rows(u, u_prev, 2)
    cw = convw_ref[...]
    y_a = zb * (cw[0:1] * u2 + cw[1:2] * u1 + cw[2:3] * u)
    u_last = u[tt - 2:tt, :]
    nconv_out[0] = u_last
    u_c[0:2, :] = u_last

    zg = jnp.dot(xnb, w_in_ref[:, OFF_GATE:D_IN], preferred_element_type=F32)
    ma_out[0] = _sigmoid(zg[:, 0:D_MODEL]) * _dot(y_a, w_pa_ref[...])
    sgb_out[0] = _sigmoid(zg[:, D_MODEL:2 * D_MODEL])

    zrkv = jnp.dot(xnb, w_in_ref[:, OFF_RKV:OFF_GATE], preferred_element_type=F32)
    zprev = _shift_rows(zrkv, zrkv_c[0:1, :], 1)
    zs = zrkv + mu_rkv_ref[...] * (zprev - zrkv)
    xprev = _shift_rows(xn, xn_c[0:1, :], 1)
    dx = xprev - xn
    mu = mu_wag_ref[...]
    xw = xn + dx * mu[0:1]
    xa = xn + dx * mu[1:2]
    xg = xn + dx * mu[2:3]
    xn_last = xn[tt - 1:tt, :]
    nshift_out[0] = xn_last
    xn_c[0:1, :] = xn_last
    zrkv_c[0:1, :] = zrkv[tt - 1:tt, :]

    wl = w0_ref[...] + _dot(jnp.tanh(_dot(xw, w1_ref[...])), w2_ref[...])
    softplus = jnp.maximum(-wl, 0.0) + jnp.log(1.0 + jnp.exp(-jnp.abs(wl)))
    lw_out[0] = -jnp.exp(-softplus - 0.5)
    a_sig = _sigmoid(a0_ref[...] + _dot(_dot(xa, a1_ref[...]), a2_ref[...]))
    g_out[0] = _dot(_sigmoid(_dot(xg, g1_ref[...])), g2_ref[...])

    r = zs[:, 0:D_RWKV]
    k = zs[:, D_RWKV:2 * D_RWKV]
    v = zs[:, 2 * D_RWKV:3 * D_RWKV]
    hsum = hsum_ref[...]
    kk = k * kk_ref[...]
    kk = kk / jnp.maximum(jnp.sqrt(_dot_hl(kk * kk, hsum)), 1e-12)
    k = k * (1.0 + (a_sig - 1.0) * ka_ref[...])
    r_out[0] = r
    k_out[0] = k
    v_out[0] = v
    a_out[0] = -kk
    b_out[0] = kk * a_sig
    bonus_out[0] = _dot_hl(r * k * rk_ref[...], hsum) * v


def _pre_call(x, b0, bsz, conv0, shift0, w, tt):
    seq = x.shape[1]
    grid = (bsz, seq // tt)
    row = lambda b, t: (b, t, 0)
    per_b = lambda b, t: (b, 0, 0)
    const2 = lambda b, t: (0, 0)

    def tok(c):
        return pl.BlockSpec((1, tt, c), row)

    def full(a):
        return pl.BlockSpec(a.shape, const2)

    weights = (w["norm1_g"], w["w_in"], w["conv_w"], w["mu_rkv"], w["mu_wag"], w["w0"], w["w1"], w["w2"],
               w["a0"], w["a1"], w["a2"], w["g1"], w["g2"], w["k_k"], w["k_a"], w["r_k"], w["w_pa"],
               w["hsum"])
    in_specs = [pl.BlockSpec((1, tt, D_MODEL), lambda b, t: (b + b0, t, 0)),
                pl.BlockSpec((1, CONV_W - 1, D_CONV), per_b),
                pl.BlockSpec((1, 1, D_MODEL), per_b)] + [full(a) for a in weights]
    tok_shape = lambda c: jax.ShapeDtypeStruct((bsz, seq, c), F32)
    out_shape = [tok_shape(D_RWKV)] * 8 + [tok_shape(D_MODEL)] * 2 + [
        jax.ShapeDtypeStruct((bsz, 1, D_MODEL), F32),
        jax.ShapeDtypeStruct((bsz, CONV_W - 1, D_CONV), F32)]
    out_specs = [tok(D_RWKV)] * 8 + [tok(D_MODEL)] * 2 + [
        pl.BlockSpec((1, 1, D_MODEL), per_b), pl.BlockSpec((1, CONV_W - 1, D_CONV), per_b)]
    return pl.pallas_call(
        _pre_kernel, grid=grid, in_specs=in_specs, out_specs=out_specs, out_shape=out_shape,
        scratch_shapes=[pltpu.VMEM((8, D_MODEL), F32), pltpu.VMEM((8, 3 * D_RWKV), F32),
                        pltpu.VMEM((8, D_CONV), F32)],
        compiler_params=pltpu.CompilerParams(dimension_semantics=("arbitrary", "arbitrary"),
                                             vmem_limit_bytes=VMEM_LIMIT_BYTES),
        name="pre",
    )(x, conv0, shift0, *weights)


def _wkv_kernel(r_ref, lw_ref, k_ref, v_ref, a_ref, b_ref, s0_ref, tri_ref,
                y_out, s_out, s_c):
    c = pl.program_id(1)
    L = r_ref.shape[1]
    pairs = range(N_PAIRS)
    lane = lax.broadcasted_iota(jnp.int32, (L, PAIR), 1)
    first = lane < HEAD_DIM
    s_row = lax.broadcasted_iota(jnp.int32, (PAIR, PAIR), 0)
    s_col = lax.broadcasted_iota(jnp.int32, (PAIR, PAIR), 1)
    s_mask = (s_row < HEAD_DIM) == (s_col < HEAD_DIM)

    @pl.when(c == 0)
    def _():
        z = jnp.zeros((HEAD_DIM, HEAD_DIM), F32)
        for p in pairs:
            s_c[p] = jnp.concatenate([jnp.concatenate([s0_ref[0, 2 * p], z], axis=1),
                                      jnp.concatenate([z, s0_ref[0, 2 * p + 1]], axis=1)], axis=0)

    def load(ref):
        return [ref[0, :, p * PAIR:(p + 1) * PAIR] for p in pairs]

    S = [s_c[p] for p in pairs]
    r, lw, k, v, a, b = (load(ref) for ref in (r_ref, lw_ref, k_ref, v_ref, a_ref, b_ref))

    tri = tri_ref[...]

    def cumsum_rows(x):
        l1 = x.astype(BF16)
        r1 = x - l1.astype(F32)
        l2 = r1.astype(BF16)
        l3 = (r1 - l2.astype(F32)).astype(BF16)
        return (jnp.dot(tri, l1, preferred_element_type=F32) + jnp.dot(tri, l2, preferred_element_type=F32)
                + jnp.dot(tri, l3, preferred_element_type=F32))

    cum = [cumsum_rows(x) for x in lw]
    cum_l = [x[L - 1:L, :] for x in cum]
    w_inv = [jnp.exp(-x) for x in cum]
    at = [a[p] * jnp.exp(cum[p] - lw[p]) for p in pairs]
    bt = [b[p] * w_inv[p] for p in pairs]
    kt = [k[p] * w_inv[p] for p in pairs]
    rt = [r[p] * jnp.exp(cum[p]) for p in pairs]
    dec = [jnp.exp(cum_l[p] - cum[p]) for p in pairs]

    row = lax.broadcasted_iota(jnp.int32, (L, L), 0)
    col = lax.broadcasted_iota(jnp.int32, (L, L), 1)
    strict = row > col
    incl = row >= col
    zero = jnp.zeros((L, L), F32)

    def per_head(x):
        return jnp.where(first, x, 0.0), jnp.where(first, 0.0, x)

    def merge(x1, x2):
        return jnp.where(first, x1, x2)

    def both(ms, x):
        return merge(_dot(ms[0], x), _dot(ms[1], x))

    at_h = [per_head(x) for x in at]
    rt_h = [per_head(x) for x in rt]
    mab = [[jnp.where(strict, _dot_nt(x, bt[p]), zero) for x in at_h[p]] for p in pairs]
    mak = [[jnp.where(strict, _dot_nt(x, kt[p]), zero) for x in at_h[p]] for p in pairs]
    nrb = [[jnp.where(incl, _dot_nt(x, bt[p]), zero) for x in rt_h[p]] for p in pairs]
    nrk = [[jnp.where(incl, _dot_nt(x, kt[p]), zero) for x in rt_h[p]] for p in pairs]

    U = [_dot_nt(at[p], S[p]) + both(mak[p], v[p]) for p in pairs]
    n = 1
    while n < L:
        U = [U[p] + both(mab[p], U[p]) for p in pairs]
        n *= 2
        if n < L:
            mab = [[_dot(m, m) for m in mab[p]] for p in pairs]
    for p in pairs:
        y_out[0, :, p * PAIR:(p + 1) * PAIR] = (_dot_nt(rt[p], S[p]) + both(nrb[p], U[p])
                                                + both(nrk[p], v[p]))
        s_new = S[p] * jnp.exp(cum_l[p]) + jnp.where(
            s_mask, _dot_tn(U[p], b[p] * dec[p]) + _dot_tn(v[p], k[p] * dec[p]), 0.0)
        s_c[p] = s_new
        s_out[0, 2 * p] = s_new[0:HEAD_DIM, 0:HEAD_DIM]
        s_out[0, 2 * p + 1] = s_new[HEAD_DIM:PAIR, HEAD_DIM:PAIR]


def _wkv_call(r, lw, k, v, a, b, s0, chunk):
    bsz, seq, _ = r.shape
    tok = pl.BlockSpec((1, chunk, D_RWKV), lambda bi, c: (bi, c, 0))
    st = pl.BlockSpec((1, HEADS, HEAD_DIM, HEAD_DIM), lambda bi, c: (bi, 0, 0, 0))
    tri = (jnp.arange(chunk)[:, None] >= jnp.arange(chunk)[None, :]).astype(BF16)
    return pl.pallas_call(
        _wkv_kernel, grid=(bsz, seq // chunk),
        in_specs=[tok] * 6 + [st, pl.BlockSpec((chunk, chunk), lambda bi, c: (0, 0))],
        out_specs=[tok, st],
        out_shape=[jax.ShapeDtypeStruct((bsz, seq, D_RWKV), F32),
                   jax.ShapeDtypeStruct((bsz, HEADS, HEAD_DIM, HEAD_DIM), F32)],
        scratch_shapes=[pltpu.VMEM((N_PAIRS, PAIR, PAIR), F32)],
        compiler_params=pltpu.CompilerParams(dimension_semantics=("arbitrary", "arbitrary")),
        name="wkv",
    )(r, lw, k, v, a, b, s0, tri)


def _post_kernel(y_ref, bonus_ref, g_ref, ma_ref, sgb_ref, x_ref, gnw_ref, gnb_ref, hsum_ref,
                 w_pb_ref, w_o_ref, n2g_ref, wq_hi_ref, wq_lo_ref,
                 x2_out, xn2_out, q_out):
    y = y_ref[...]
    hsum = hsum_ref[...]
    mean = _dot_hl(y, hsum) * (1.0 / HEAD_DIM)
    d = y - mean
    var = _dot_hl(d * d, hsum) * (1.0 / HEAD_DIM)
    yn = d * lax.rsqrt(var + GN_EPS) * gnw_ref[...] + gnb_ref[...] + bonus_ref[...]
    y_b = yn * g_ref[...]
    merged = ma_ref[...] + sgb_ref[...] * _dot(y_b, w_pb_ref[...])
    x2 = x_ref[...] + _dot(merged, w_o_ref[...])
    x2_out[...] = x2
    xn2 = _rms_norm(x2, n2g_ref[...])
    xn2_out[...] = xn2
    hi, lo = _split(xn2)
    wq_hi = wq_hi_ref[...]
    q_out[...] = (jnp.dot(hi, wq_hi, preferred_element_type=F32)
                  + jnp.dot(lo, wq_hi, preferred_element_type=F32)
                  + jnp.dot(hi, wq_lo_ref[...], preferred_element_type=F32))


def _post_call(y, bonus, g, ma, sgb, x, x_row0, w, tt):
    n = y.shape[0]
    row = lambda i: (i, 0)
    const = lambda i: (0, 0)
    tok = lambda c: pl.BlockSpec((tt, c), row)
    weights = (w["gn_w"], w["gn_b"], w["hsum"], w["w_pb"], w["w_o"], w["norm2_g"], w["wq_hi"], w["wq_lo"])
    d_q = w["wq_hi"].shape[1]
    return pl.pallas_call(
        _post_kernel, grid=(n // tt,),
        in_specs=[tok(D_RWKV)] * 3 + [tok(D_MODEL)] * 2 + [
            pl.BlockSpec((tt, D_MODEL), lambda i: (i + x_row0 // tt, 0))] + [
            pl.BlockSpec(a.shape, const) for a in weights],
        out_specs=[tok(D_MODEL), tok(D_MODEL), tok(d_q)],
        out_shape=[jax.ShapeDtypeStruct((n, D_MODEL), F32)] * 2 + [jax.ShapeDtypeStruct((n, d_q), F32)],
        compiler_params=pltpu.CompilerParams(dimension_semantics=("arbitrary",),
                                             vmem_limit_bytes=VMEM_LIMIT_BYTES),
        name="post",
    )(y, bonus, g, ma, sgb, x, *weights)


TOPK_HEADS = 4
STAIR_COUNTS = tuple(PEER_TOPK // (a + 1) for a in range(8))
STAIR_ROWS = 16 + 8 * 7 + 8


def _extract_max(s, iota, n_rows):
    m = jnp.max(s, axis=0, keepdims=True)
    idx = jnp.min(jnp.where(s == m, iota, n_rows), axis=0, keepdims=True)
    return m, idx, iota == idx


def _topk_kernel(q_ref, khi_ref, klo_ref, e_out, g_out, s_scr, v_scr, i_scr, c_scr, ci_scr, sc_scr):
    nt = lambda x, y: lax.dot_general(x, y, (((1,), (1,)), ((), ())), preferred_element_type=F32)
    for c in range(2 * TOPK_HEADS):
        h, p = divmod(c, 2)
        q_hi, q_lo = _split(q_ref[:, c * PEER_HALF:(c + 1) * PEER_HALF])
        k_hi = khi_ref[h, p]
        s_scr[c] = nt(k_hi, q_hi) + nt(k_hi, q_lo) + nt(klo_ref[h, p], q_hi)

    iota = lax.broadcasted_iota(jnp.int32, (PEER_KEYS, TOK_TILE), 0)

    def sub_key_step(j, carry):
        for c in range(2 * TOPK_HEADS):
            s = s_scr[c]
            m, idx, hit = _extract_max(s, iota, PEER_KEYS)
            v_scr[c, pl.ds(j, 1), :] = m
            i_scr[c, pl.ds(j, 1), :] = idx
            s_scr[c] = jnp.where(hit, -jnp.inf, s)
        return carry

    lax.fori_loop(0, PEER_TOPK, sub_key_step, 0)

    row8 = lax.broadcasted_iota(jnp.int32, (8, TOK_TILE), 0)
    for h in range(TOPK_HEADS):
        v1, i1 = v_scr[2 * h], i_scr[2 * h] * PEER_KEYS
        v2, i2 = v_scr[2 * h + 1], i_scr[2 * h + 1]
        vals = [v1[0:1] + v2]
        idxs = [i1[0:1] + i2]
        for a in range(1, 8):
            vals.append(jnp.where(row8 < STAIR_COUNTS[a], v1[a:a + 1] + v2[0:8], -jnp.inf))
            idxs.append(i1[a:a + 1] + i2[0:8])
        vals.append(v1[8:16] + v2[0:1])
        idxs.append(i1[8:16] + i2[0:1])
        c_scr[h] = jnp.concatenate(vals, axis=0)
        ci_scr[h] = jnp.concatenate(idxs, axis=0)

    iota_c = lax.broadcasted_iota(jnp.int32, (STAIR_ROWS, TOK_TILE), 0)

    def expert_step(j, carry):
        for h in range(TOPK_HEADS):
            s = c_scr[h]
            m, _, hit = _extract_max(s, iota_c, STAIR_ROWS)
            sc_scr[h, pl.ds(j, 1), :] = m
            e_out[pl.ds(h * PEER_TOPK + j, 1), :] = jnp.max(jnp.where(hit, ci_scr[h], -1), axis=0,
                                                             keepdims=True)
            c_scr[h] = jnp.where(hit, -jnp.inf, s)
        return carry

    lax.fori_loop(0, PEER_TOPK, expert_step, 0)

    for h in range(TOPK_HEADS):
        sc = sc_scr[h]
        e = jnp.exp(sc - sc[0:1])
        g_out[h * PEER_TOPK:(h + 1) * PEER_TOPK, :] = e / jnp.sum(e, axis=0, keepdims=True)


def _topk_call(q, khi, klo):
    n = q.shape[0]
    rows = TOPK_HEADS * PEER_TOPK
    sel = pl.BlockSpec((rows, TOK_TILE), lambda i, h: (h, i))
    keys = pl.BlockSpec((TOPK_HEADS, 2, PEER_KEYS, PEER_HALF), lambda i, h: (h, 0, 0, 0))
    chains = 2 * TOPK_HEADS
    return pl.pallas_call(
        _topk_kernel, grid=(n // TOK_TILE, PEER_HEADS // TOPK_HEADS),
        in_specs=[pl.BlockSpec((TOK_TILE, chains * PEER_HALF), lambda i, h: (i, h)), keys, keys],
        out_specs=[sel, sel],
        out_shape=[jax.ShapeDtypeStruct((PEER_SEL, n), jnp.int32), jax.ShapeDtypeStruct((PEER_SEL, n), F32)],
        scratch_shapes=[pltpu.VMEM((chains, PEER_KEYS, TOK_TILE), F32),
                        pltpu.VMEM((chains, PEER_TOPK, TOK_TILE), F32),
                        pltpu.VMEM((chains, PEER_TOPK, TOK_TILE), jnp.int32),
                        pltpu.VMEM((TOPK_HEADS, STAIR_ROWS, TOK_TILE), F32),
                        pltpu.VMEM((TOPK_HEADS, STAIR_ROWS, TOK_TILE), jnp.int32),
                        pltpu.VMEM((TOPK_HEADS, PEER_TOPK, TOK_TILE), F32)],
        compiler_params=pltpu.CompilerParams(dimension_semantics=("arbitrary", "arbitrary")),
        name="topk",
    )(q, khi, klo)


def _pack_kernel(u_ref, v_ref, w_out):
    ub = lax.bitcast_convert_type(u_ref[...].astype(BF16).astype(F32), jnp.uint32)
    vb = lax.bitcast_convert_type(v_ref[...].astype(BF16).astype(F32), jnp.uint32)
    w_out[...] = (ub & jnp.uint32(0xFFFF0000)) | (vb >> 16)


def _pack_call(u, v):
    n, d = u.shape
    rows = 512
    blk = pl.BlockSpec((rows, d), lambda i: (i, 0))
    return pl.pallas_call(
        _pack_kernel, grid=(n // rows,), in_specs=[blk, blk], out_specs=blk,
        out_shape=jax.ShapeDtypeStruct((n, d), jnp.uint32),
        compiler_params=pltpu.CompilerParams(dimension_semantics=("arbitrary",)),
        name="pack",
    )(u, v)


def _mix_tokens(plane, xn2_rows, gate_rows, x2_rows, nfg, between):
    n = len(xn2_rows)
    ones = jnp.ones((128, 128), BF16)
    sums = [None] * n
    step = 0
    for s in range(8):
        for k in range(n):
            u = lax.bitcast_convert_type(plane(k, s) & jnp.uint32(0xFFFF0000), F32)
            term = u * xn2_rows[k][:, s * 128:(s + 1) * 128]
            sums[k] = term if s == 0 else sums[k] + term
            between(step)
            step += 1
    coefs = []
    for k in range(n):
        act = _dot_hl(sums[k], ones).T[0:8, :]
        gelu = 0.5 * act * (1.0 + jnp.tanh(0.7978845608028654 * (act + 0.044715 * (act * act * act))))
        coef = gate_rows[k] * gelu
        coefs.append(jnp.broadcast_to(coef[0:1, :], (128, PEER_SEL)).T)
    outs = [[] for _ in range(n)]
    for s in range(8):
        for k in range(n):
            v = lax.bitcast_convert_type(plane(k, s) << 16, F32)
            mix = jnp.sum(coefs[k] * v, axis=0, keepdims=True)
            outs[k].append(x2_rows[k][:, s * 128:(s + 1) * 128] + mix)
            between(step)
            step += 1
    return [_rms_norm(jnp.concatenate(outs[k], axis=1), nfg) for k in range(n)]


def _peer_kernel(e_ref, g_ref, xn2_ref, x2_ref, nfg_ref, w_hbm, y_out,
                 e_smem, g_vmem, wbuf, sem_e, sem_w):
    to_smem = pltpu.make_async_copy(e_ref, e_smem, sem_e)
    to_smem.start()
    g_vmem[...] = g_ref[...].T
    to_smem.wait()

    def row_copy(t, slot, j):
        return pltpu.make_async_copy(w_hbm.at[e_smem[j, t]], wbuf.at[slot, :, j, :], sem_w.at[slot])

    def wait_slot(slot):
        pltpu.make_async_copy(wbuf.at[(slot + 1) % PEER_SLOTS], wbuf.at[slot], sem_w.at[slot]).wait()

    def mix_pair(t, slots, t_next, next_slots):
        starts = [(k, j) for k in range(2) for j in range(PEER_SEL)]
        per_step = len(starts) // 32

        def start_some(step):
            for k, j in starts[step * per_step:(step + 1) * per_step]:
                row_copy(t_next + k, next_slots[k], j).start(priority=j % 2)

        rows = lambda ref: [ref[pl.ds(t + k, 1), :] for k in range(2)]
        ys = _mix_tokens(lambda k, s: wbuf[slots[k], s], rows(xn2_ref), rows(g_vmem), rows(x2_ref),
                         nfg_ref[...], start_some)
        for k in range(2):
            y_out[pl.ds(t + k, 1), :] = ys[k]

    for j in range(PEER_SEL):
        row_copy(0, 0, j).start(priority=j % 2)
        row_copy(1, 1, j).start(priority=j % 2)

    def body(i, carry):
        t = PEER_SLOTS * i
        wait_slot(0)
        wait_slot(1)
        mix_pair(t, (0, 1), t + 2, (2, 3))
        wait_slot(2)
        wait_slot(3)
        mix_pair(t + 2, (2, 3), jnp.minimum(t + 4, TOK_TILE - 2), (0, 1))
        return carry

    lax.fori_loop(0, TOK_TILE // PEER_SLOTS, body, 0)
    wait_slot(0)
    wait_slot(1)


def _peer_call(eidx, gate, xn2, x2, nfg, w3, first_tile, n_tiles):
    sel = pl.BlockSpec((PEER_SEL, TOK_TILE), lambda i: (0, i + first_tile))
    tok = pl.BlockSpec((TOK_TILE, D_MODEL), lambda i: (i + first_tile, 0))
    return pl.pallas_call(
        _peer_kernel, grid=(n_tiles,),
        in_specs=[sel, sel, tok, tok, pl.BlockSpec((1, D_MODEL), lambda i: (0, 0)),
                  pl.BlockSpec(memory_space=pl.ANY)],
        out_specs=pl.BlockSpec((TOK_TILE, D_MODEL), lambda i: (i, 0)),
        out_shape=jax.ShapeDtypeStruct((n_tiles * TOK_TILE, D_MODEL), F32),
        scratch_shapes=[pltpu.SMEM((PEER_SEL, TOK_TILE), jnp.int32),
                        pltpu.VMEM((TOK_TILE, PEER_SEL), F32),
                        pltpu.VMEM((PEER_SLOTS, 8, PEER_SEL, 128), jnp.uint32),
                        pltpu.SemaphoreType.DMA,
                        pltpu.SemaphoreType.DMA((PEER_SLOTS,))],
        compiler_params=pltpu.CompilerParams(dimension_semantics=("arbitrary",)),
        name="peer",
    )(eidx, gate, xn2, x2, nfg, w3)


def _sc_gather_call(w3, idx):
    rows = idx.shape[0]
    per_worker = rows // (SC_CORES * SC_SUBCORES)
    mesh = plsc.VectorSubcoreMesh(core_axis_name="c", subcore_axis_name="s",
                                  num_cores=SC_CORES, num_subcores=SC_SUBCORES)

    @functools.partial(
        pl.kernel, mesh=mesh, out_type=jax.ShapeDtypeStruct((rows, 8, 128), jnp.uint32),
        scratch_types=[pltpu.VMEM((SC_WINDOW,), jnp.int32),
                       pltpu.VMEM((SC_WINDOW, 8, 128), jnp.uint32),
                       pltpu.SemaphoreType.DMA],
        name="sc_gather")
    def gather(table_hbm, idx_hbm, out_hbm, idx_v, rows_v, sem):
        base = (lax.axis_index("s") * SC_CORES + lax.axis_index("c")) * per_worker

        @pl.loop(0, per_worker // SC_WINDOW)
        def _(win):
            off = base + win * SC_WINDOW
            pltpu.sync_copy(idx_hbm.at[pl.ds(off, SC_WINDOW)], idx_v)
            pltpu.async_copy(table_hbm.at[idx_v], rows_v, sem).wait()
            pltpu.sync_copy(rows_v, out_hbm.at[pl.ds(off, SC_WINDOW)])

    return gather(w3, idx)


def _peer_staged_kernel(g_ref, xn2_ref, x2_ref, nfg_ref, w_ref, y_out, g_vmem):
    sub = pl.program_id(1)

    @pl.when(sub == 0)
    def _():
        g_vmem[...] = g_ref[...].T

    plane = lambda k, s: w_ref[pl.ds(k * (8 * PEER_SEL) + s, PEER_SEL, stride=8), :]
    rows = lambda ref: [ref[k:k + 1, :] for k in range(STAGE_TOKENS)]
    gates = [g_vmem[pl.ds(sub * STAGE_TOKENS + k, 1), :] for k in range(STAGE_TOKENS)]
    ys = _mix_tokens(plane, rows(xn2_ref), gates, rows(x2_ref), nfg_ref[...], lambda step: None)
    y_out[...] = jnp.concatenate(ys, axis=0)


def _peer_staged_call(gate, xn2, x2, nfg, staged, first_tile, n_tiles):
    subs = TOK_TILE // STAGE_TOKENS
    tok = pl.BlockSpec((STAGE_TOKENS, D_MODEL), lambda i, j: ((i + first_tile) * subs + j, 0))
    staged2d = staged.reshape(-1, 128)
    return pl.pallas_call(
        _peer_staged_kernel, grid=(n_tiles, subs),
        in_specs=[pl.BlockSpec((PEER_SEL, TOK_TILE), lambda i, j: (0, i + first_tile)), tok, tok,
                  pl.BlockSpec((1, D_MODEL), lambda i, j: (0, 0)),
                  pl.BlockSpec((STAGE_TOKENS * PEER_SEL * 8, 128),
                               lambda i, j: (i * subs + j, 0))],
        out_specs=pl.BlockSpec((STAGE_TOKENS, D_MODEL), lambda i, j: (i * subs + j, 0)),
        out_shape=jax.ShapeDtypeStruct((n_tiles * TOK_TILE, D_MODEL), F32),
        scratch_shapes=[pltpu.VMEM((TOK_TILE, PEER_SEL), F32)],
        compiler_params=pltpu.CompilerParams(dimension_semantics=("arbitrary", "arbitrary"),
                                             vmem_limit_bytes=VMEM_LIMIT_BYTES),
        name="peer_staged",
    )(gate, xn2, x2, nfg, staged2d)


def _tile_choices(bsz, seq):
    tt = min(seq, 256)
    chunk = min(seq, 64)
    post = min(bsz * seq, 256)
    return tt, chunk, post


def _mixer_stage(x, b0, bsz, st_conv, st_shift, st_wkv, w):
    seq = x.shape[1]
    n = bsz * seq
    tt, chunk, post_tt = _tile_choices(bsz, seq)
    (r, lw, k, v, a, b, bonus, g, ma, sgb, new_shift, new_conv) = _pre_call(
        x, b0, bsz, st_conv, st_shift.reshape(bsz, 1, D_MODEL), w, tt)
    y, new_wkv = _wkv_call(r, lw, k, v, a, b, st_wkv, chunk)
    flat = lambda t: t.reshape(-1, t.shape[-1])
    x2, xn2, q = _post_call(flat(y), flat(bonus), flat(g), flat(ma), flat(sgb), flat(x), b0 * seq, w,
                            post_tt)
    eidx, gate = _topk_call(q, w["keys_hi"], w["keys_lo"])
    states = (new_conv[None], new_shift.reshape(1, bsz, D_MODEL), new_wkv[None])
    return ((bsz, seq, D_MODEL), eidx, gate, xn2, x2), states


def _peer_stage(sel, w):
    (bsz, seq, _), eidx, gate, xn2, x2 = sel
    tiles = bsz * seq // TOK_TILE
    if tiles < SC_MIN_TILES:
        out = _peer_call(eidx, gate, xn2, x2, w["norm_f_g"], w["peer_w"], 0, tiles)
        return out.reshape(bsz, seq, D_MODEL)
    bounds = list(range(0, tiles, SC_CHUNK_TILES)) + [tiles]
    staged = [_sc_gather_call(w["peer_w"], eidx[:, lo * TOK_TILE:hi * TOK_TILE].T.reshape(-1))
              for lo, hi in zip(bounds[:-1], bounds[1:])]
    pieces = [_peer_staged_call(gate, xn2, x2, w["norm_f_g"], rows, lo, hi - lo)
              for rows, lo, hi in zip(staged, bounds[:-1], bounds[1:])]
    return jnp.concatenate(pieces).reshape(bsz, seq, D_MODEL)


def kernel(x_prompt, x_sample, state_conv, state_shift, state_wkv, norm1_g, w_in, conv_w, mu_rkv, mu_wag,
           w0, w1, w2, a0, a1, a2, g1, g2, k_k, k_a, r_k, gn_w, gn_b, w_pa, w_pb, w_o, norm2_g,
           peer_wq, peer_keys, peer_u, peer_v, norm_f_g):
    row = lambda t: t.reshape(1, -1)
    head = jnp.arange(D_RWKV) // HEAD_DIM
    wq_hi, wq_lo = _split(peer_wq[0])
    keys = peer_keys[0]
    keys_hi, keys_lo = _split(keys)
    w = dict(
        norm1_g=norm1_g, w_in=w_in[0].astype(BF16), conv_w=conv_w[0], mu_rkv=mu_rkv, mu_wag=mu_wag[0],
        w0=w0, w1=w1[0].astype(BF16), w2=w2[0].astype(BF16), a0=a0, a1=a1[0].astype(BF16),
        a2=a2[0].astype(BF16), g1=g1[0].astype(BF16), g2=g2[0].astype(BF16), k_k=k_k, k_a=k_a,
        r_k=row(r_k[0]), gn_w=gn_w, gn_b=gn_b, w_pa=w_pa[0].astype(BF16), w_pb=w_pb[0].astype(BF16),
        w_o=w_o[0].astype(BF16), norm2_g=norm2_g, wq_hi=wq_hi, wq_lo=wq_lo, keys_hi=keys_hi,
        keys_lo=keys_lo, norm_f_g=row(norm_f_g),
        peer_w=_pack_call(peer_u[0], peer_v[0]).reshape(-1, 8, 128),
        hsum=(head[:, None] == head[None, :]).astype(BF16),
    )
    bp = x_prompt.shape[0]
    per = bp // PROMPT_GROUPS
    zero_conv = jnp.zeros((per, CONV_W - 1, D_CONV), F32)
    zero_shift = jnp.zeros((per, D_MODEL), F32)
    zero_wkv = jnp.zeros((per, HEADS, HEAD_DIM, HEAD_DIM), F32)
    groups = [(x_sample, 0, x_sample.shape[0], state_conv[0], state_shift[0], state_wkv[0])]
    groups += [(x_prompt, i * per, per, zero_conv, zero_shift, zero_wkv) for i in range(PROMPT_GROUPS)]
    mixed = [_mixer_stage(*grp, w) for grp in groups]
    ys = [_peer_stage(sel, w) for sel, _ in mixed]
    (conv_s, shift_s, wkv_s), prompt_states = mixed[0][1], [st for _, st in mixed[1:]]
    conv_p, shift_p, wkv_p = (jnp.concatenate(parts, axis=1) for parts in zip(*prompt_states))
    return (jnp.concatenate(ys[1:], axis=0), ys[0], conv_p, shift_p, wkv_p, conv_s, shift_s, wkv_s)
```

```python
import functools

import jax
import jax.numpy as jnp
from jax import lax
from jax.experimental import pallas as pl
from jax.experimental.pallas import tpu as pltpu
from jax.experimental.pallas import tpu_sc as plsc

F32 = jnp.float32
BF16 = jnp.bfloat16

D_MODEL = 1024
D_CONV = 512
CONV_W = 3
HEADS = 8
HEAD_DIM = 64
D_RWKV = HEADS * HEAD_DIM
PAIR = 2 * HEAD_DIM
N_PAIRS = HEADS // 2
GN_EPS = 64e-5
RMS_EPS = 1e-6
OFF_RKV = 3 * D_CONV
OFF_GATE = OFF_RKV + 3 * D_RWKV
D_IN = OFF_GATE + 2 * D_MODEL

PEER_HEADS = 8
PEER_KEYS = 128
PEER_HALF = 128
PEER_TOPK = 16
PEER_SEL = PEER_HEADS * PEER_TOPK
TOK_TILE = 128
PEER_SLOTS = 4
SC_CORES = 2
SC_SUBCORES = 16
SC_WINDOW = 64
SC_CHUNK_TILES = 32
PROMPT_GROUPS = 8
SC_MIN_TILES = 32
STAGE_TOKENS = 32

VMEM_LIMIT_BYTES = 56 * 1024 * 1024


def _dot(a, b):
    return jnp.dot(a.astype(BF16), b.astype(BF16), preferred_element_type=F32)


def _dot_nt(a, b):
    return lax.dot_general(a.astype(BF16), b.astype(BF16), (((1,), (1,)), ((), ())),
                           preferred_element_type=F32)


def _dot_tn(a, b):
    return lax.dot_general(a.astype(BF16), b.astype(BF16), (((0,), (0,)), ((), ())),
                           preferred_element_type=F32)


def _split(a):
    hi = a.astype(BF16)
    lo = (a - hi.astype(F32)).astype(BF16)
    return hi, lo


def _dot_hl(a, w_bf16):
    hi, lo = _split(a)
    return (jnp.dot(hi, w_bf16, preferred_element_type=F32)
            + jnp.dot(lo, w_bf16, preferred_element_type=F32))


def _sigmoid(x):
    return 1.0 / (1.0 + jnp.exp(-x))


def _rms_norm(x, g):
    return x * lax.rsqrt(jnp.mean(x * x, axis=-1, keepdims=True) + RMS_EPS) * g


def _shift_rows(a, carry, n):
    rolled = pltpu.roll(a, n, 0)
    row = lax.broadcasted_iota(jnp.int32, a.shape, 0)
    for i in range(n):
        rolled = jnp.where(row == i, carry[i:i + 1], rolled)
    return rolled


def _pre_kernel(x_ref, conv0_ref, shift0_ref, n1g_ref, w_in_ref, convw_ref, mu_rkv_ref, mu_wag_ref,
                w0_ref, w1_ref, w2_ref, a0_ref, a1_ref, a2_ref, g1_ref, g2_ref, kk_ref, ka_ref, rk_ref,
                w_pa_ref, hsum_ref,
                r_out, lw_out, k_out, v_out, a_out, b_out, bonus_out, g_out, ma_out, sgb_out,
                nshift_out, nconv_out,
                xn_c, zrkv_c, u_c):
    t = pl.program_id(1)
    tt = x_ref.shape[1]
    xn = _rms_norm(x_ref[0], n1g_ref[...])
    xnb = xn.astype(BF16)

    @pl.when(t == 0)
    def _():
        prev = jnp.broadcast_to(shift0_ref[0], (8, D_MODEL))
        xn_c[...] = prev
        zrkv_c[...] = jnp.dot(prev.astype(BF16), w_in_ref[:, OFF_RKV:OFF_GATE],
                              preferred_element_type=F32)
        u_c[0:2, :] = conv0_ref[0]

    zbch = jnp.dot(xnb, w_in_ref[:, 0:OFF_RKV], preferred_element_type=F32)
    zb = zbch[:, 0:D_CONV]
    u = zbch[:, D_CONV:2 * D_CONV] * zbch[:, 2 * D_CONV:3 * D_CONV]
    u_prev = u_c[0:2, :]
    u1 = _shift_rows(u, u_prev[1:2], 1)
    u2 = _shift_rows(u, u_prev, 2)
    cw = convw_ref[...]
    y_a = zb * (cw[0:1] * u2 + cw[1:2] * u1 + cw[2:3] * u)
    u_last = u[tt - 2:tt, :]
    nconv_out[0] = u_last
    u_c[0:2, :] = u_last

    zg = jnp.dot(xnb, w_in_ref[:, OFF_GATE:D_IN], preferred_element_type=F32)
    ma_out[0] = _sigmoid(zg[:, 0:D_MODEL]) * _dot(y_a, w_pa_ref[...])
    sgb_out[0] = _sigmoid(zg[:, D_MODEL:2 * D_MODEL])

    zrkv = jnp.dot(xnb, w_in_ref[:, OFF_RKV:OFF_GATE], preferred_element_type=F32)
    zprev = _shift_rows(zrkv, zrkv_c[0:1, :], 1)
    zs = zrkv + mu_rkv_ref[...] * (zprev - zrkv)
    xprev = _shift_rows(xn, xn_c[0:1, :], 1)
    dx = xprev - xn
    mu = mu_wag_ref[...]
    xw = xn + dx * mu[0:1]
    xa = xn + dx * mu[1:2]
    xg = xn + dx * mu[2:3]
    xn_last = xn[tt - 1:tt, :]
    nshift_out[0] = xn_last
    xn_c[0:1, :] = xn_last
    zrkv_c[0:1, :] = zrkv[tt - 1:tt, :]

    wl = w0_ref[...] + _dot(jnp.tanh(_dot(xw, w1_ref[...])), w2_ref[...])
    softplus = jnp.maximum(-wl, 0.0) + jnp.log(1.0 + jnp.exp(-jnp.abs(wl)))
    lw_out[0] = -jnp.exp(-softplus - 0.5)
    a_sig = _sigmoid(a0_ref[...] + _dot(_dot(xa, a1_ref[...]), a2_ref[...]))
    g_out[0] = _dot(_sigmoid(_dot(xg, g1_ref[...])), g2_ref[...])

    r = zs[:, 0:D_RWKV]
    k = zs[:, D_RWKV:2 * D_RWKV]
    v = zs[:, 2 * D_RWKV:3 * D_RWKV]
    hsum = hsum_ref[...]
    kk = k * kk_ref[...]
    kk = kk / jnp.maximum(jnp.sqrt(_dot_hl(kk * kk, hsum)), 1e-12)
    k = k * (1.0 + (a_sig - 1.0) * ka_ref[...])
    r_out[0] = r
    k_out[0] = k
    v_out[0] = v
    a_out[0] = -kk
    b_out[0] = kk * a_sig
    bonus_out[0] = _dot_hl(r * k * rk_ref[...], hsum) * v


def _pre_call(x, b0, bsz, conv0, shift0, w, tt):
    seq = x.shape[1]
    grid = (bsz, seq // tt)
    row = lambda b, t: (b, t, 0)
    per_b = lambda b, t: (b, 0, 0)
    const2 = lambda b, t: (0, 0)

    def tok(c):
        return pl.BlockSpec((1, tt, c), row)

    def full(a):
        return pl.BlockSpec(a.shape, const2)

    weights = (w["norm1_g"], w["w_in"], w["conv_w"], w["mu_rkv"], w["mu_wag"], w["w0"], w["w1"], w["w2"],
               w["a0"], w["a1"], w["a2"], w["g1"], w["g2"], w["k_k"], w["k_a"], w["r_k"], w["w_pa"],
               w["hsum"])
    in_specs = [pl.BlockSpec((1, tt, D_MODEL), lambda b, t: (b + b0, t, 0)),
                pl.BlockSpec((1, CONV_W - 1, D_CONV), per_b),
                pl.BlockSpec((1, 1, D_MODEL), per_b)] + [full(a) for a in weights]
    tok_shape = lambda c: jax.ShapeDtypeStruct((bsz, seq, c), F32)
    out_shape = [tok_shape(D_RWKV)] * 8 + [tok_shape(D_MODEL)] * 2 + [
        jax.ShapeDtypeStruct((bsz, 1, D_MODEL), F32),
        jax.ShapeDtypeStruct((bsz, CONV_W - 1, D_CONV), F32)]
    out_specs = [tok(D_RWKV)] * 8 + [tok(D_MODEL)] * 2 + [
        pl.BlockSpec((1, 1, D_MODEL), per_b), pl.BlockSpec((1, CONV_W - 1, D_CONV), per_b)]
    return pl.pallas_call(
        _pre_kernel, grid=grid, in_specs=in_specs, out_specs=out_specs, out_shape=out_shape,
        scratch_shapes=[pltpu.VMEM((8, D_MODEL), F32), pltpu.VMEM((8, 3 * D_RWKV), F32),
                        pltpu.VMEM((8, D_CONV), F32)],
        compiler_params=pltpu.CompilerParams(dimension_semantics=("arbitrary", "arbitrary"),
                                             vmem_limit_bytes=VMEM_LIMIT_BYTES),
        name="pre",
    )(x, conv0, shift0, *weights)


def _wkv_kernel(r_ref, lw_ref, k_ref, v_ref, a_ref, b_ref, s0_ref, tri_ref,
                y_out, s_out, s_c):
    c = pl.program_id(1)
    L = r_ref.shape[1]
    pairs = range(N_PAIRS)
    lane = lax.broadcasted_iota(jnp.int32, (L, PAIR), 1)
    first = lane < HEAD_DIM
    s_row = lax.broadcasted_iota(jnp.int32, (PAIR, PAIR), 0)
    s_col = lax.broadcasted_iota(jnp.int32, (PAIR, PAIR), 1)
    s_mask = (s_row < HEAD_DIM) == (s_col < HEAD_DIM)

    @pl.when(c == 0)
    def _():
        z = jnp.zeros((HEAD_DIM, HEAD_DIM), F32)
        for p in pairs:
            s_c[p] = jnp.concatenate([jnp.concatenate([s0_ref[0, 2 * p], z], axis=1),
                                      jnp.concatenate([z, s0_ref[0, 2 * p + 1]], axis=1)], axis=0)

    def load(ref):
        return [ref[0, :, p * PAIR:(p + 1) * PAIR] for p in pairs]

    S = [s_c[p] for p in pairs]
    r, lw, k, v, a, b = (load(ref) for ref in (r_ref, lw_ref, k_ref, v_ref, a_ref, b_ref))

    tri = tri_ref[...]

    def cumsum_rows(x):
        l1 = x.astype(BF16)
        r1 = x - l1.astype(F32)
        l2 = r1.astype(BF16)
        l3 = (r1 - l2.astype(F32)).astype(BF16)
        return (jnp.dot(tri, l1, preferred_element_type=F32) + jnp.dot(tri, l2, preferred_element_type=F32)
                + jnp.dot(tri, l3, preferred_element_type=F32))

    cum = [cumsum_rows(x) for x in lw]
    cum_l = [x[L - 1:L, :] for x in cum]
    w_inv = [jnp.exp(-x) for x in cum]
    at = [a[p] * jnp.exp(cum[p] - lw[p]) for p in pairs]
    bt = [b[p] * w_inv[p] for p in pairs]
    kt = [k[p] * w_inv[p] for p in pairs]
    rt = [r[p] * jnp.exp(cum[p]) for p in pairs]
    dec = [jnp.exp(cum_l[p] - cum[p]) for p in pairs]

    row = lax.broadcasted_iota(jnp.int32, (L, L), 0)
    col = lax.broadcasted_iota(jnp.int32, (L, L), 1)
    strict = row > col
    incl = row >= col
    zero = jnp.zeros((L, L), F32)

    def per_head(x):
        return jnp.where(first, x, 0.0), jnp.where(first, 0.0, x)

    def merge(x1, x2):
        return jnp.where(first, x1, x2)

    def both(ms, x):
        return merge(_dot(ms[0], x), _dot(ms[1], x))

    at_h = [per_head(x) for x in at]
    rt_h = [per_head(x) for x in rt]
    mab = [[jnp.where(strict, _dot_nt(x, bt[p]), zero) for x in at_h[p]] for p in pairs]
    mak = [[jnp.where(strict, _dot_nt(x, kt[p]), zero) for x in at_h[p]] for p in pairs]
    nrb = [[jnp.where(incl, _dot_nt(x, bt[p]), zero) for x in rt_h[p]] for p in pairs]
    nrk = [[jnp.where(incl, _dot_nt(x, kt[p]), zero) for x in rt_h[p]] for p in pairs]

    U = [_dot_nt(at[p], S[p]) + both(mak[p], v[p]) for p in pairs]
    n = 1
    while n < L:
        U = [U[p] + both(mab[p], U[p]) for p in pairs]
        n *= 2
        if n < L:
            mab = [[_dot(m, m) for m in mab[p]] for p in pairs]
    for p in pairs:
        y_out[0, :, p * PAIR:(p + 1) * PAIR] = (_dot_nt(rt[p], S[p]) + both(nrb[p], U[p])
                                                + both(nrk[p], v[p]))
        s_new = S[p] * jnp.exp(cum_l[p]) + jnp.where(
            s_mask, _dot_tn(U[p], b[p] * dec[p]) + _dot_tn(v[p], k[p] * dec[p]), 0.0)
        s_c[p] = s_new
        s_out[0, 2 * p] = s_new[0:HEAD_DIM, 0:HEAD_DIM]
        s_out[0, 2 * p + 1] = s_new[HEAD_DIM:PAIR, HEAD_DIM:PAIR]


def _wkv_call(r, lw, k, v, a, b, s0, chunk):
    bsz, seq, _ = r.shape
    tok = pl.BlockSpec((1, chunk, D_RWKV), lambda bi, c: (bi, c, 0))
    st = pl.BlockSpec((1, HEADS, HEAD_DIM, HEAD_DIM), lambda bi, c: (bi, 0, 0, 0))
    tri = (jnp.arange(chunk)[:, None] >= jnp.arange(chunk)[None, :]).astype(BF16)
    return pl.pallas_call(
        _wkv_kernel, grid=(bsz, seq // chunk),
        in_specs=[tok] * 6 + [st, pl.BlockSpec((chunk, chunk), lambda bi, c: (0, 0))],
        out_specs=[tok, st],
        out_shape=[jax.ShapeDtypeStruct((bsz, seq, D_RWKV), F32),
                   jax.ShapeDtypeStruct((bsz, HEADS, HEAD_DIM, HEAD_DIM), F32)],
        scratch_shapes=[pltpu.VMEM((N_PAIRS, PAIR, PAIR), F32)],
        compiler_params=pltpu.CompilerParams(dimension_semantics=("arbitrary", "arbitrary")),
        name="wkv",
    )(r, lw, k, v, a, b, s0, tri)


def _post_kernel(y_ref, bonus_ref, g_ref, ma_ref, sgb_ref, x_ref, gnw_ref, gnb_ref, hsum_ref,
                 w_pb_ref, w_o_ref, n2g_ref, wq_hi_ref, wq_lo_ref,
                 x2_out, xn2_out, q_out):
    y = y_ref[...]
    hsum = hsum_ref[...]
    mean = _dot_hl(y, hsum) * (1.0 / HEAD_DIM)
    d = y - mean
    var = _dot_hl(d * d, hsum) * (1.0 / HEAD_DIM)
    yn = d * lax.rsqrt(var + GN_EPS) * gnw_ref[...] + gnb_ref[...] + bonus_ref[...]
    y_b = yn * g_ref[...]
    merged = ma_ref[...] + sgb_ref[...] * _dot(y_b, w_pb_ref[...])
    x2 = x_ref[...] + _dot(merged, w_o_ref[...])
    x2_out[...] = x2
    xn2 = _rms_norm(x2, n2g_ref[...])
    xn2_out[...] = xn2
    hi, lo = _split(xn2)
    wq_hi = wq_hi_ref[...]
    q_out[...] = (jnp.dot(hi, wq_hi, preferred_element_type=F32)
                  + jnp.dot(lo, wq_hi, preferred_element_type=F32)
                  + jnp.dot(hi, wq_lo_ref[...], preferred_element_type=F32))


def _post_call(y, bonus, g, ma, sgb, x, x_row0, w, tt):
    n = y.shape[0]
    row = lambda i: (i, 0)
    const = lambda i: (0, 0)
    tok = lambda c: pl.BlockSpec((tt, c), row)
    weights = (w["gn_w"], w["gn_b"], w["hsum"], w["w_pb"], w["w_o"], w["norm2_g"], w["wq_hi"], w["wq_lo"])
    d_q = w["wq_hi"].shape[1]
    return pl.pallas_call(
        _post_kernel, grid=(n // tt,),
        in_specs=[tok(D_RWKV)] * 3 + [tok(D_MODEL)] * 2 + [
            pl.BlockSpec((tt, D_MODEL), lambda i: (i + x_row0 // tt, 0))] + [
            pl.BlockSpec(a.shape, const) for a in weights],
        out_specs=[tok(D_MODEL), tok(D_MODEL), tok(d_q)],
        out_shape=[jax.ShapeDtypeStruct((n, D_MODEL), F32)] * 2 + [jax.ShapeDtypeStruct((n, d_q), F32)],
        compiler_params=pltpu.CompilerParams(dimension_semantics=("arbitrary",),
                                             vmem_limit_bytes=VMEM_LIMIT_BYTES),
        name="post",
    )(y, bonus, g, ma, sgb, x, *weights)


TOPK_HEADS = 8
STAIR_COUNTS = tuple(PEER_TOPK // (a + 1) for a in range(8))
STAIR_ROWS = 16 + 8 * 7 + 8


def _extract_max(s, iota, n_rows):
    m = jnp.max(s, axis=0, keepdims=True)
    idx = jnp.min(jnp.where(s == m, iota, n_rows), axis=0, keepdims=True)
    return m, idx, iota == idx


def _topk_kernel(q_ref, khi_ref, klo_ref, e_out, g_out, s_scr, v_scr, i_scr, c_scr, ci_scr, sc_scr):
    nt = lambda x, y: lax.dot_general(x, y, (((1,), (1,)), ((), ())), preferred_element_type=F32)
    for c in range(2 * TOPK_HEADS):
        h, p = divmod(c, 2)
        q_hi, q_lo = _split(q_ref[:, c * PEER_HALF:(c + 1) * PEER_HALF])
        k_hi = khi_ref[h, p]
        s_scr[c] = nt(k_hi, q_hi) + nt(k_hi, q_lo) + nt(klo_ref[h, p], q_hi)

    iota = lax.broadcasted_iota(jnp.int32, (PEER_KEYS, TOK_TILE), 0)

    def sub_key_step(j, carry):
        for c in range(2 * TOPK_HEADS):
            s = s_scr[c]
            m, idx, hit = _extract_max(s, iota, PEER_KEYS)
            v_scr[c, pl.ds(j, 1), :] = m
            i_scr[c, pl.ds(j, 1), :] = idx
            s_scr[c] = jnp.where(hit, -jnp.inf, s)
        return carry

    lax.fori_loop(0, PEER_TOPK, sub_key_step, 0)

    row8 = lax.broadcasted_iota(jnp.int32, (8, TOK_TILE), 0)
    for h in range(TOPK_HEADS):
        v1, i1 = v_scr[2 * h], i_scr[2 * h] * PEER_KEYS
        v2, i2 = v_scr[2 * h + 1], i_scr[2 * h + 1]
        vals = [v1[0:1] + v2]
        idxs = [i1[0:1] + i2]
        for a in range(1, 8):
            vals.append(jnp.where(row8 < STAIR_COUNTS[a], v1[a:a + 1] + v2[0:8], -jnp.inf))
            idxs.append(i1[a:a + 1] + i2[0:8])
        vals.append(v1[8:16] + v2[0:1])
        idxs.append(i1[8:16] + i2[0:1])
        c_scr[h] = jnp.concatenate(vals, axis=0)
        ci_scr[h] = jnp.concatenate(idxs, axis=0)

    iota_c = lax.broadcasted_iota(jnp.int32, (STAIR_ROWS, TOK_TILE), 0)

    def expert_step(j, carry):
        for h in range(TOPK_HEADS):
            s = c_scr[h]
            m, _, hit = _extract_max(s, iota_c, STAIR_ROWS)
            sc_scr[h, pl.ds(j, 1), :] = m
            e_out[pl.ds(h * PEER_TOPK + j, 1), :] = jnp.max(jnp.where(hit, ci_scr[h], -1), axis=0,
                                                             keepdims=True)
            c_scr[h] = jnp.where(hit, -jnp.inf, s)
        return carry

    lax.fori_loop(0, PEER_TOPK, expert_step, 0)

    for h in range(TOPK_HEADS):
        sc = sc_scr[h]
        e = jnp.exp(sc - sc[0:1])
        g_out[h * PEER_TOPK:(h + 1) * PEER_TOPK, :] = e / jnp.sum(e, axis=0, keepdims=True)


def _topk_call(q, khi, klo):
    n = q.shape[0]
    rows = TOPK_HEADS * PEER_TOPK
    sel = pl.BlockSpec((rows, TOK_TILE), lambda i, h: (h, i))
    keys = pl.BlockSpec((TOPK_HEADS, 2, PEER_KEYS, PEER_HALF), lambda i, h: (h, 0, 0, 0))
    chains = 2 * TOPK_HEADS
    return pl.pallas_call(
        _topk_kernel, grid=(n // TOK_TILE, PEER_HEADS // TOPK_HEADS),
        in_specs=[pl.BlockSpec((TOK_TILE, chains * PEER_HALF), lambda i, h: (i, h)), keys, keys],
        out_specs=[sel, sel],
        out_shape=[jax.ShapeDtypeStruct((PEER_SEL, n), jnp.int32), jax.ShapeDtypeStruct((PEER_SEL, n), F32)],
        scratch_shapes=[pltpu.VMEM((chains, PEER_KEYS, TOK_TILE), F32),
                        pltpu.VMEM((chains, PEER_TOPK, TOK_TILE), F32),
                        pltpu.VMEM((chains, PEER_TOPK, TOK_TILE), jnp.int32),
                        pltpu.VMEM((TOPK_HEADS, STAIR_ROWS, TOK_TILE), F32),
                        pltpu.VMEM((TOPK_HEADS, STAIR_ROWS, TOK_TILE), jnp.int32),
                        pltpu.VMEM((TOPK_HEADS, PEER_TOPK, TOK_TILE), F32)],
        compiler_params=pltpu.CompilerParams(dimension_semantics=("arbitrary", "arbitrary")),
        name="topk",
    )(q, khi, klo)


def _pack_kernel(u_ref, v_ref, w_out):
    ub = lax.bitcast_convert_type(u_ref[...].astype(BF16).astype(F32), jnp.uint32)
    vb = lax.bitcast_convert_type(v_ref[...].astype(BF16).astype(F32), jnp.uint32)
    w_out[...] = (ub & jnp.uint32(0xFFFF0000)) | (vb >> 16)


def _pack_call(u, v):
    n, d = u.shape
    rows = 512
    blk = pl.BlockSpec((rows, d), lambda i: (i, 0))
    return pl.pallas_call(
        _pack_kernel, grid=(n // rows,), in_specs=[blk, blk], out_specs=blk,
        out_shape=jax.ShapeDtypeStruct((n, d), jnp.uint32),
        compiler_params=pltpu.CompilerParams(dimension_semantics=("arbitrary",)),
        name="pack",
    )(u, v)


def _mix_tokens(plane, xn2_rows, gate_rows, x2_rows, nfg, between):
    n = len(xn2_rows)
    ones = jnp.ones((128, 128), BF16)
    sums = [None] * n
    step = 0
    for s in range(8):
        for k in range(n):
            u = lax.bitcast_convert_type(plane(k, s) & jnp.uint32(0xFFFF0000), F32)
            term = u * xn2_rows[k][:, s * 128:(s + 1) * 128]
            sums[k] = term if s == 0 else sums[k] + term
            between(step)
            step += 1
    coefs = []
    for k in range(n):
        act = _dot_hl(sums[k], ones).T[0:8, :]
        gelu = 0.5 * act * (1.0 + jnp.tanh(0.7978845608028654 * (act + 0.044715 * (act * act * act))))
        coef = gate_rows[k] * gelu
        coefs.append(jnp.broadcast_to(coef[0:1, :], (128, PEER_SEL)).T)
    outs = [[] for _ in range(n)]
    for s in range(8):
        for k in range(n):
            v = lax.bitcast_convert_type(plane(k, s) << 16, F32)
            mix = jnp.sum(coefs[k] * v, axis=0, keepdims=True)
            outs[k].append(x2_rows[k][:, s * 128:(s + 1) * 128] + mix)
            between(step)
            step += 1
    return [_rms_norm(jnp.concatenate(outs[k], axis=1), nfg) for k in range(n)]


def _peer_kernel(e_ref, g_ref, xn2_ref, x2_ref, nfg_ref, w_hbm, y_out,
                 e_smem, g_vmem, wbuf, sem_e, sem_w):
    to_smem = pltpu.make_async_copy(e_ref, e_smem, sem_e)
    to_smem.start()
    g_vmem[...] = g_ref[...].T
    to_smem.wait()

    def row_copy(t, slot, j):
        return pltpu.make_async_copy(w_hbm.at[e_smem[j, t]], wbuf.at[slot, :, j, :], sem_w.at[slot])

    def wait_slot(slot):
        pltpu.make_async_copy(wbuf.at[(slot + 1) % PEER_SLOTS], wbuf.at[slot], sem_w.at[slot]).wait()

    def mix_pair(t, slots, t_next, next_slots):
        starts = [(k, j) for k in range(2) for j in range(PEER_SEL)]
        per_step = len(starts) // 32

        def start_some(step):
            for k, j in starts[step * per_step:(step + 1) * per_step]:
                row_copy(t_next + k, next_slots[k], j).start(priority=j % 2)

        rows = lambda ref: [ref[pl.ds(t + k, 1), :] for k in range(2)]
        ys = _mix_tokens(lambda k, s: wbuf[slots[k], s], rows(xn2_ref), rows(g_vmem), rows(x2_ref),
                         nfg_ref[...], start_some)
        for k in range(2):
            y_out[pl.ds(t + k, 1), :] = ys[k]

    for j in range(PEER_SEL):
        row_copy(0, 0, j).start(priority=j % 2)
        row_copy(1, 1, j).start(priority=j % 2)

    def body(i, carry):
        t = PEER_SLOTS * i
        wait_slot(0)
        wait_slot(1)
        mix_pair(t, (0, 1), t + 2, (2, 3))
        wait_slot(2)
        wait_slot(3)
        mix_pair(t + 2, (2, 3), jnp.minimum(t + 4, TOK_TILE - 2), (0, 1))
        return carry

    lax.fori_loop(0, TOK_TILE // PEER_SLOTS, body, 0)
    wait_slot(0)
    wait_slot(1)


def _peer_call(eidx, gate, xn2, x2, nfg, w3, first_tile, n_tiles):
    sel = pl.BlockSpec((PEER_SEL, TOK_TILE), lambda i: (0, i + first_tile))
    tok = pl.BlockSpec((TOK_TILE, D_MODEL), lambda i: (i + first_tile, 0))
    return pl.pallas_call(
        _peer_kernel, grid=(n_tiles,),
        in_specs=[sel, sel, tok, tok, pl.BlockSpec((1, D_MODEL), lambda i: (0, 0)),
                  pl.BlockSpec(memory_space=pl.ANY)],
        out_specs=pl.BlockSpec((TOK_TILE, D_MODEL), lambda i: (i, 0)),
        out_shape=jax.ShapeDtypeStruct((n_tiles * TOK_TILE, D_MODEL), F32),
        scratch_shapes=[pltpu.SMEM((PEER_SEL, TOK_TILE), jnp.int32),
                        pltpu.VMEM((TOK_TILE, PEER_SEL), F32),
                        pltpu.VMEM((PEER_SLOTS, 8, PEER_SEL, 128), jnp.uint32),
                        pltpu.SemaphoreType.DMA,
                        pltpu.SemaphoreType.DMA((PEER_SLOTS,))],
        compiler_params=pltpu.CompilerParams(dimension_semantics=("arbitrary",)),
        name="peer",
    )(eidx, gate, xn2, x2, nfg, w3)


def _sc_gather_call(w3, idx):
    rows = idx.shape[0]
    per_worker = rows // (SC_CORES * SC_SUBCORES)
    mesh = plsc.VectorSubcoreMesh(core_axis_name="c", subcore_axis_name="s",
                                  num_cores=SC_CORES, num_subcores=SC_SUBCORES)

    @functools.partial(
        pl.kernel, mesh=mesh, out_type=jax.ShapeDtypeStruct((rows, 8, 128), jnp.uint32),
        scratch_types=[pltpu.VMEM((SC_WINDOW,), jnp.int32),
                       pltpu.VMEM((SC_WINDOW, 8, 128), jnp.uint32),
                       pltpu.SemaphoreType.DMA],
        name="sc_gather")
    def gather(table_hbm, idx_hbm, out_hbm, idx_v, rows_v, sem):
        base = (lax.axis_index("s") * SC_CORES + lax.axis_index("c")) * per_worker

        @pl.loop(0, per_worker // SC_WINDOW)
        def _(win):
            off = base + win * SC_WINDOW
            pltpu.sync_copy(idx_hbm.at[pl.ds(off, SC_WINDOW)], idx_v)
            pltpu.async_copy(table_hbm.at[idx_v], rows_v, sem).wait()
            pltpu.sync_copy(rows_v, out_hbm.at[pl.ds(off, SC_WINDOW)])

    return gather(w3, idx)


def _peer_staged_kernel(g_ref, xn2_ref, x2_ref, nfg_ref, w_ref, y_out, g_vmem):
    sub = pl.program_id(1)

    @pl.when(sub == 0)
    def _():
        g_vmem[...] = g_ref[...].T

    plane = lambda k, s: w_ref[pl.ds(k * (8 * PEER_SEL) + s, PEER_SEL, stride=8), :]
    rows = lambda ref: [ref[k:k + 1, :] for k in range(STAGE_TOKENS)]
    gates = [g_vmem[pl.ds(sub * STAGE_TOKENS + k, 1), :] for k in range(STAGE_TOKENS)]
    ys = _mix_tokens(plane, rows(xn2_ref), gates, rows(x2_ref), nfg_ref[...], lambda step: None)
    y_out[...] = jnp.concatenate(ys, axis=0)


def _peer_staged_call(gate, xn2, x2, nfg, staged, first_tile, n_tiles):
    subs = TOK_TILE // STAGE_TOKENS
    tok = pl.BlockSpec((STAGE_TOKENS, D_MODEL), lambda i, j: ((i + first_tile) * subs + j, 0))
    staged2d = staged.reshape(-1, 128)
    return pl.pallas_call(
        _peer_staged_kernel, grid=(n_tiles, subs),
        in_specs=[pl.BlockSpec((PEER_SEL, TOK_TILE), lambda i, j: (0, i + first_tile)), tok, tok,
                  pl.BlockSpec((1, D_MODEL), lambda i, j: (0, 0)),
                  pl.BlockSpec((STAGE_TOKENS * PEER_SEL * 8, 128),
                               lambda i, j: (i * subs + j, 0))],
        out_specs=pl.BlockSpec((STAGE_TOKENS, D_MODEL), lambda i, j: (i * subs + j, 0)),
        out_shape=jax.ShapeDtypeStruct((n_tiles * TOK_TILE, D_MODEL), F32),
        scratch_shapes=[pltpu.VMEM((TOK_TILE, PEER_SEL), F32)],
        compiler_params=pltpu.CompilerParams(dimension_semantics=("arbitrary", "arbitrary"),
                                             vmem_limit_bytes=VMEM_LIMIT_BYTES),
        name="peer_staged",
    )(gate, xn2, x2, nfg, staged2d)


def _tile_choices(bsz, seq):
    tt = min(seq, 256)
    chunk = min(seq, 64)
    post = min(bsz * seq, 256)
    return tt, chunk, post


def _mixer_stage(x, b0, bsz, st_conv, st_shift, st_wkv, w):
    seq = x.shape[1]
    n = bsz * seq
    tt, chunk, post_tt = _tile_choices(bsz, seq)
    (r, lw, k, v, a, b, bonus, g, ma, sgb, new_shift, new_conv) = _pre_call(
        x, b0, bsz, st_conv, st_shift.reshape(bsz, 1, D_MODEL), w, tt)
    y, new_wkv = _wkv_call(r, lw, k, v, a, b, st_wkv, chunk)
    flat = lambda t: t.reshape(-1, t.shape[-1])
    x2, xn2, q = _post_call(flat(y), flat(bonus), flat(g), flat(ma), flat(sgb), flat(x), b0 * seq, w,
                            post_tt)
    eidx, gate = _topk_call(q, w["keys_hi"], w["keys_lo"])
    states = (new_conv[None], new_shift.reshape(1, bsz, D_MODEL), new_wkv[None])
    return ((bsz, seq, D_MODEL), eidx, gate, xn2, x2), states


def _peer_stage(sel, w):
    (bsz, seq, _), eidx, gate, xn2, x2 = sel
    tiles = bsz * seq // TOK_TILE
    if tiles < SC_MIN_TILES:
        out = _peer_call(eidx, gate, xn2, x2, w["norm_f_g"], w["peer_w"], 0, tiles)
        return out.reshape(bsz, seq, D_MODEL)
    bounds = list(range(0, tiles, SC_CHUNK_TILES)) + [tiles]
    staged = [_sc_gather_call(w["peer_w"], eidx[:, lo * TOK_TILE:hi * TOK_TILE].T.reshape(-1))
              for lo, hi in zip(bounds[:-1], bounds[1:])]
    pieces = [_peer_staged_call(gate, xn2, x2, w["norm_f_g"], rows, lo, hi - lo)
              for rows, lo, hi in zip(staged, bounds[:-1], bounds[1:])]
    return jnp.concatenate(pieces).reshape(bsz, seq, D_MODEL)


def kernel(x_prompt, x_sample, state_conv, state_shift, state_wkv, norm1_g, w_in, conv_w, mu_rkv, mu_wag,
           w0, w1, w2, a0, a1, a2, g1, g2, k_k, k_a, r_k, gn_w, gn_b, w_pa, w_pb, w_o, norm2_g,
           peer_wq, peer_keys, peer_u, peer_v, norm_f_g):
    row = lambda t: t.reshape(1, -1)
    head = jnp.arange(D_RWKV) // HEAD_DIM
    wq_hi, wq_lo = _split(peer_wq[0])
    keys = peer_keys[0]
    keys_hi, keys_lo = _split(keys)
    w = dict(
        norm1_g=norm1_g, w_in=w_in[0].astype(BF16), conv_w=conv_w[0], mu_rkv=mu_rkv, mu_wag=mu_wag[0],
        w0=w0, w1=w1[0].astype(BF16), w2=w2[0].astype(BF16), a0=a0, a1=a1[0].astype(BF16),
        a2=a2[0].astype(BF16), g1=g1[0].astype(BF16), g2=g2[0].astype(BF16), k_k=k_k, k_a=k_a,
        r_k=row(r_k[0]), gn_w=gn_w, gn_b=gn_b, w_pa=w_pa[0].astype(BF16), w_pb=w_pb[0].astype(BF16),
        w_o=w_o[0].astype(BF16), norm2_g=norm2_g, wq_hi=wq_hi, wq_lo=wq_lo, keys_hi=keys_hi,
        keys_lo=keys_lo, norm_f_g=row(norm_f_g),
        peer_w=_pack_call(peer_u[0], peer_v[0]).reshape(-1, 8, 128),
        hsum=(head[:, None] == head[None, :]).astype(BF16),
    )
    bp = x_prompt.shape[0]
    per = bp // PROMPT_GROUPS
    zero_conv = jnp.zeros((per, CONV_W - 1, D_CONV), F32)
    zero_shift = jnp.zeros((per, D_MODEL), F32)
    zero_wkv = jnp.zeros((per, HEADS, HEAD_DIM, HEAD_DIM), F32)
    groups = [(x_sample, 0, x_sample.shape[0], state_conv[0], state_shift[0], state_wkv[0])]
    groups += [(x_prompt, i * per, per, zero_conv, zero_shift, zero_wkv) for i in range(PROMPT_GROUPS)]
    mixed = [_mixer_stage(*grp, w) for grp in groups]
    ys = [_peer_stage(sel, w) for sel, _ in mixed]
    (conv_s, shift_s, wkv_s), prompt_states = mixed[0][1], [st for _, st in mixed[1:]]
    conv_p, shift_p, wkv_p = (jnp.concatenate(parts, axis=1) for parts in zip(*prompt_states))
    return (jnp.concatenate(ys[1:], axis=0), ys[0], conv_p, shift_p, wkv_p, conv_s, shift_s, wkv_s)
```

```python
import functools

import jax
import jax.numpy as jnp
from jax import lax
from jax.experimental import pallas as pl
from jax.experimental.pallas import tpu as pltpu
from jax.experimental.pallas import tpu_sc as plsc

F32 = jnp.float32
BF16 = jnp.bfloat16

D_MODEL = 1024
D_CONV = 512
CONV_W = 3
HEADS = 8
HEAD_DIM = 64
D_RWKV = HEADS * HEAD_DIM
PAIR = 2 * HEAD_DIM
N_PAIRS = HEADS // 2
GN_EPS = 64e-5
RMS_EPS = 1e-6
OFF_RKV = 3 * D_CONV
OFF_GATE = OFF_RKV + 3 * D_RWKV
D_IN = OFF_GATE + 2 * D_MODEL

PEER_HEADS = 8
PEER_KEYS = 128
PEER_HALF = 128
PEER_TOPK = 16
PEER_SEL = PEER_HEADS * PEER_TOPK
TOK_TILE = 128
PEER_SLOTS = 4
SC_CORES = 2
SC_SUBCORES = 16
SC_WINDOW = 64
SC_CHUNK_TILES = 32
PROMPT_GROUPS = 8
SC_MIN_TILES = 32
STAGE_TOKENS = 32

VMEM_LIMIT_BYTES = 56 * 1024 * 1024


def _dot(a, b):
    return jnp.dot(a.astype(BF16), b.astype(BF16), preferred_element_type=F32)


def _dot_nt(a, b):
    return lax.dot_general(a.astype(BF16), b.astype(BF16), (((1,), (1,)), ((), ())),
                           preferred_element_type=F32)


def _dot_tn(a, b):
    return lax.dot_general(a.astype(BF16), b.astype(BF16), (((0,), (0,)), ((), ())),
                           preferred_element_type=F32)


def _split(a):
    hi = a.astype(BF16)
    lo = (a - hi.astype(F32)).astype(BF16)
    return hi, lo


def _dot_hl(a, w_bf16):
    hi, lo = _split(a)
    return (jnp.dot(hi, w_bf16, preferred_element_type=F32)
            + jnp.dot(lo, w_bf16, preferred_element_type=F32))


def _sigmoid(x):
    return 1.0 / (1.0 + jnp.exp(-x))


def _rms_norm(x, g):
    return x * lax.rsqrt(jnp.mean(x * x, axis=-1, keepdims=True) + RMS_EPS) * g


def _shift_rows(a, carry, n):
    rolled = pltpu.roll(a, n, 0)
    row = lax.broadcasted_iota(jnp.int32, a.shape, 0)
    for i in range(n):
        rolled = jnp.where(row == i, carry[i:i + 1], rolled)
    return rolled


def _pre_kernel(x_ref, conv0_ref, shift0_ref, n1g_ref, w_in_ref, convw_ref, mu_rkv_ref, mu_wag_ref,
                w0_ref, w1_ref, w2_ref, a0_ref, a1_ref, a2_ref, g1_ref, g2_ref, kk_ref, ka_ref, rk_ref,
                w_pa_ref, hsum_ref,
                r_out, lw_out, k_out, v_out, a_out, b_out, bonus_out, g_out, ma_out, sgb_out,
                nshift_out, nconv_out,
                xn_c, zrkv_c, u_c):
    t = pl.program_id(1)
    tt = x_ref.shape[1]
    xn = _rms_norm(x_ref[0], n1g_ref[...])
    xnb = xn.astype(BF16)

    @pl.when(t == 0)
    def _():
        prev = jnp.broadcast_to(shift0_ref[0], (8, D_MODEL))
        xn_c[...] = prev
        zrkv_c[...] = jnp.dot(prev.astype(BF16), w_in_ref[:, OFF_RKV:OFF_GATE],
                              preferred_element_type=F32)
        u_c[0:2, :] = conv0_ref[0]

    zbch = jnp.dot(xnb, w_in_ref[:, 0:OFF_RKV], preferred_element_type=F32)
    zb = zbch[:, 0:D_CONV]
    u = zbch[:, D_CONV:2 * D_CONV] * zbch[:, 2 * D_CONV:3 * D_CONV]
    u_prev = u_c[0:2, :]
    u1 = _shift_rows(u, u_prev[1:2], 1)
    u2 = _shift_rows(u, u_prev, 2)
    cw = convw_ref[...]
    y_a = zb * (cw[0:1] * u2 + cw[1:2] * u1 + cw[2:3] * u)
    u_last = u[tt - 2:tt, :]
    nconv_out[0] = u_last
    u_c[0:2, :] = u_last

    zg = jnp.dot(xnb, w_in_ref[:, OFF_GATE:D_IN], preferred_element_type=F32)
    ma_out[0] = _sigmoid(zg[:, 0:D_MODEL]) * _dot(y_a, w_pa_ref[...])
    sgb_out[0] = _sigmoid(zg[:, D_MODEL:2 * D_MODEL])

    zrkv = jnp.dot(xnb, w_in_ref[:, OFF_RKV:OFF_GATE], preferred_element_type=F32)
    zprev = _shift_rows(zrkv, zrkv_c[0:1, :], 1)
    zs = zrkv + mu_rkv_ref[...] * (zprev - zrkv)
    xprev = _shift_rows(xn, xn_c[0:1, :], 1)
    dx = xprev - xn
    mu = mu_wag_ref[...]
    xw = xn + dx * mu[0:1]
    xa = xn + dx * mu[1:2]
    xg = xn + dx * mu[2:3]
    xn_last = xn[tt - 1:tt, :]
    nshift_out[0] = xn_last
    xn_c[0:1, :] = xn_last
    zrkv_c[0:1, :] = zrkv[tt - 1:tt, :]

    wl = w0_ref[...] + _dot(jnp.tanh(_dot(xw, w1_ref[...])), w2_ref[...])
    softplus = jnp.maximum(-wl, 0.0) + jnp.log(1.0 + jnp.exp(-jnp.abs(wl)))
    lw_out[0] = -jnp.exp(-softplus - 0.5)
    a_sig = _sigmoid(a0_ref[...] + _dot(_dot(xa, a1_ref[...]), a2_ref[...]))
    g_out[0] = _dot(_sigmoid(_dot(xg, g1_ref[...])), g2_ref[...])

    r = zs[:, 0:D_RWKV]
    k = zs[:, D_RWKV:2 * D_RWKV]
    v = zs[:, 2 * D_RWKV:3 * D_RWKV]
    hsum = hsum_ref[...]
    kk = k * kk_ref[...]
    kk = kk / jnp.maximum(jnp.sqrt(_dot_hl(kk * kk, hsum)), 1e-12)
    k = k * (1.0 + (a_sig - 1.0) * ka_ref[...])
    r_out[0] = r
    k_out[0] = k
    v_out[0] = v
    a_out[0] = -kk
    b_out[0] = kk * a_sig
    bonus_out[0] = _dot_hl(r * k * rk_ref[...], hsum) * v


def _pre_call(x, b0, bsz, conv0, shift0, w, tt):
    seq = x.shape[1]
    grid = (bsz, seq // tt)
    row = lambda b, t: (b, t, 0)
    per_b = lambda b, t: (b, 0, 0)
    const2 = lambda b, t: (0, 0)

    def tok(c):
        return pl.BlockSpec((1, tt, c), row)

    def full(a):
        return pl.BlockSpec(a.shape, const2)

    weights = (w["norm1_g"], w["w_in"], w["conv_w"], w["mu_rkv"], w["mu_wag"], w["w0"], w["w1"], w["w2"],
               w["a0"], w["a1"], w["a2"], w["g1"], w["g2"], w["k_k"], w["k_a"], w["r_k"], w["w_pa"],
               w["hsum"])
    in_specs = [pl.BlockSpec((1, tt, D_MODEL), lambda b, t: (b + b0, t, 0)),
                pl.BlockSpec((1, CONV_W - 1, D_CONV), per_b),
                pl.BlockSpec((1, 1, D_MODEL), per_b)] + [full(a) for a in weights]
    tok_shape = lambda c: jax.ShapeDtypeStruct((bsz, seq, c), F32)
    out_shape = [tok_shape(D_RWKV)] * 8 + [tok_shape(D_MODEL)] * 2 + [
        jax.ShapeDtypeStruct((bsz, 1, D_MODEL), F32),
        jax.ShapeDtypeStruct((bsz, CONV_W - 1, D_CONV), F32)]
    out_specs = [tok(D_RWKV)] * 8 + [tok(D_MODEL)] * 2 + [
        pl.BlockSpec((1, 1, D_MODEL), per_b), pl.BlockSpec((1, CONV_W - 1, D_CONV), per_b)]
    return pl.pallas_call(
        _pre_kernel, grid=grid, in_specs=in_specs, out_specs=out_specs, out_shape=out_shape,
        scratch_shapes=[pltpu.VMEM((8, D_MODEL), F32), pltpu.VMEM((8, 3 * D_RWKV), F32),
                        pltpu.VMEM((8, D_CONV), F32)],
        compiler_params=pltpu.CompilerParams(dimension_semantics=("arbitrary", "arbitrary"),
                                             vmem_limit_bytes=VMEM_LIMIT_BYTES),
        name="pre",
    )(x, conv0, shift0, *weights)


def _wkv_kernel(r_ref, lw_ref, k_ref, v_ref, a_ref, b_ref, s0_ref, tri_ref,
                y_out, s_out, s_c):
    c = pl.program_id(1)
    L = r_ref.shape[1]
    pairs = range(N_PAIRS)
    lane = lax.broadcasted_iota(jnp.int32, (L, PAIR), 1)
    first = lane < HEAD_DIM
    s_row = lax.broadcasted_iota(jnp.int32, (PAIR, PAIR), 0)
    s_col = lax.broadcasted_iota(jnp.int32, (PAIR, PAIR), 1)
    s_mask = (s_row < HEAD_DIM) == (s_col < HEAD_DIM)

    @pl.when(c == 0)
    def _():
        z = jnp.zeros((HEAD_DIM, HEAD_DIM), F32)
        for p in pairs:
            s_c[p] = jnp.concatenate([jnp.concatenate([s0_ref[0, 2 * p], z], axis=1),
                                      jnp.concatenate([z, s0_ref[0, 2 * p + 1]], axis=1)], axis=0)

    def load(ref):
        return [ref[0, :, p * PAIR:(p + 1) * PAIR] for p in pairs]

    S = [s_c[p] for p in pairs]
    r, lw, k, v, a, b = (load(ref) for ref in (r_ref, lw_ref, k_ref, v_ref, a_ref, b_ref))

    tri = tri_ref[...]

    def cumsum_rows(x):
        l1 = x.astype(BF16)
        r1 = x - l1.astype(F32)
        l2 = r1.astype(BF16)
        l3 = (r1 - l2.astype(F32)).astype(BF16)
        return (jnp.dot(tri, l1, preferred_element_type=F32) + jnp.dot(tri, l2, preferred_element_type=F32)
                + jnp.dot(tri, l3, preferred_element_type=F32))

    cum = [cumsum_rows(x) for x in lw]
    cum_l = [x[L - 1:L, :] for x in cum]
    w_inv = [jnp.exp(-x) for x in cum]
    at = [a[p] * jnp.exp(cum[p] - lw[p]) for p in pairs]
    bt = [b[p] * w_inv[p] for p in pairs]
    kt = [k[p] * w_inv[p] for p in pairs]
    rt = [r[p] * jnp.exp(cum[p]) for p in pairs]
    dec = [jnp.exp(cum_l[p] - cum[p]) for p in pairs]

    row = lax.broadcasted_iota(jnp.int32, (L, L), 0)
    col = lax.broadcasted_iota(jnp.int32, (L, L), 1)
    strict = row > col
    incl = row >= col
    zero = jnp.zeros((L, L), F32)

    def per_head(x):
        return jnp.where(first, x, 0.0), jnp.where(first, 0.0, x)

    def merge(x1, x2):
        return jnp.where(first, x1, x2)

    def both(ms, x):
        return merge(_dot(ms[0], x), _dot(ms[1], x))

    at_h = [per_head(x) for x in at]
    rt_h = [per_head(x) for x in rt]
    mab = [[jnp.where(strict, _dot_nt(x, bt[p]), zero) for x in at_h[p]] for p in pairs]
    mak = [[jnp.where(strict, _dot_nt(x, kt[p]), zero) for x in at_h[p]] for p in pairs]
    nrb = [[jnp.where(incl, _dot_nt(x, bt[p]), zero) for x in rt_h[p]] for p in pairs]
    nrk = [[jnp.where(incl, _dot_nt(x, kt[p]), zero) for x in rt_h[p]] for p in pairs]

    U = [_dot_nt(at[p], S[p]) + both(mak[p], v[p]) for p in pairs]
    n = 1
    while n < L:
        U = [U[p] + both(mab[p], U[p]) for p in pairs]
        n *= 2
        if n < L:
            mab = [[_dot(m, m) for m in mab[p]] for p in pairs]
    for p in pairs:
        y_out[0, :, p * PAIR:(p + 1) * PAIR] = (_dot_nt(rt[p], S[p]) + both(nrb[p], U[p])
                                                + both(nrk[p], v[p]))
        s_new = S[p] * jnp.exp(cum_l[p]) + jnp.where(
            s_mask, _dot_tn(U[p], b[p] * dec[p]) + _dot_tn(v[p], k[p] * dec[p]), 0.0)
        s_c[p] = s_new
        s_out[0, 2 * p] = s_new[0:HEAD_DIM, 0:HEAD_DIM]
        s_out[0, 2 * p + 1] = s_new[HEAD_DIM:PAIR, HEAD_DIM:PAIR]


def _wkv_call(r, lw, k, v, a, b, s0, chunk):
    bsz, seq, _ = r.shape
    tok = pl.BlockSpec((1, chunk, D_RWKV), lambda bi, c: (bi, c, 0))
    st = pl.BlockSpec((1, HEADS, HEAD_DIM, HEAD_DIM), lambda bi, c: (bi, 0, 0, 0))
    tri = (jnp.arange(chunk)[:, None] >= jnp.arange(chunk)[None, :]).astype(BF16)
    return pl.pallas_call(
        _wkv_kernel, grid=(bsz, seq // chunk),
        in_specs=[tok] * 6 + [st, pl.BlockSpec((chunk, chunk), lambda bi, c: (0, 0))],
        out_specs=[tok, st],
        out_shape=[jax.ShapeDtypeStruct((bsz, seq, D_RWKV), F32),
                   jax.ShapeDtypeStruct((bsz, HEADS, HEAD_DIM, HEAD_DIM), F32)],
        scratch_shapes=[pltpu.VMEM((N_PAIRS, PAIR, PAIR), F32)],
        compiler_params=pltpu.CompilerParams(dimension_semantics=("arbitrary", "arbitrary")),
        name="wkv",
    )(r, lw, k, v, a, b, s0, tri)


def _post_kernel(y_ref, bonus_ref, g_ref, ma_ref, sgb_ref, x_ref, gnw_ref, gnb_ref, hsum_ref,
                 w_pb_ref, w_o_ref, n2g_ref, wq_hi_ref, wq_lo_ref,
                 x2_out, q_out):
    y = y_ref[...]
    hsum = hsum_ref[...]
    mean = _dot_hl(y, hsum) * (1.0 / HEAD_DIM)
    d = y - mean
    var = _dot_hl(d * d, hsum) * (1.0 / HEAD_DIM)
    yn = d * lax.rsqrt(var + GN_EPS) * gnw_ref[...] + gnb_ref[...] + bonus_ref[...]
    y_b = yn * g_ref[...]
    merged = ma_ref[...] + sgb_ref[...] * _dot(y_b, w_pb_ref[...])
    x2 = x_ref[...] + _dot(merged, w_o_ref[...])
    x2_out[...] = x2
    hi, lo = _split(_rms_norm(x2, n2g_ref[...]))
    wq_hi = wq_hi_ref[...]
    q_out[...] = (jnp.dot(hi, wq_hi, preferred_element_type=F32)
                  + jnp.dot(lo, wq_hi, preferred_element_type=F32)
                  + jnp.dot(hi, wq_lo_ref[...], preferred_element_type=F32))


def _post_call(y, bonus, g, ma, sgb, x, x_row0, w, tt):
    n = y.shape[0]
    row = lambda i: (i, 0)
    const = lambda i: (0, 0)
    tok = lambda c: pl.BlockSpec((tt, c), row)
    weights = (w["gn_w"], w["gn_b"], w["hsum"], w["w_pb"], w["w_o"], w["norm2_g"], w["wq_hi"], w["wq_lo"])
    d_q = w["wq_hi"].shape[1]
    return pl.pallas_call(
        _post_kernel, grid=(n // tt,),
        in_specs=[tok(D_RWKV)] * 3 + [tok(D_MODEL)] * 2 + [
            pl.BlockSpec((tt, D_MODEL), lambda i: (i + x_row0 // tt, 0))] + [
            pl.BlockSpec(a.shape, const) for a in weights],
        out_specs=[tok(D_MODEL), tok(d_q)],
        out_shape=[jax.ShapeDtypeStruct((n, D_MODEL), F32), jax.ShapeDtypeStruct((n, d_q), F32)],
        compiler_params=pltpu.CompilerParams(dimension_semantics=("arbitrary",),
                                             vmem_limit_bytes=VMEM_LIMIT_BYTES),
        name="post",
    )(y, bonus, g, ma, sgb, x, *weights)


TOPK_HEADS = 8
STAIR_COUNTS = tuple(PEER_TOPK // (a + 1) for a in range(8))
STAIR_ROWS = 16 + 8 * 7 + 8


def _extract_max(s, iota, n_rows):
    m = jnp.max(s, axis=0, keepdims=True)
    idx = jnp.min(jnp.where(s == m, iota, n_rows), axis=0, keepdims=True)
    return m, idx, iota == idx


def _topk_kernel(q_ref, khi_ref, klo_ref, e_out, g_out, s_scr, v_scr, i_scr, c_scr, ci_scr, sc_scr):
    nt = lambda x, y: lax.dot_general(x, y, (((1,), (1,)), ((), ())), preferred_element_type=F32)
    for c in range(2 * TOPK_HEADS):
        h, p = divmod(c, 2)
        q_hi, q_lo = _split(q_ref[:, c * PEER_HALF:(c + 1) * PEER_HALF])
        k_hi = khi_ref[h, p]
        s_scr[c] = nt(k_hi, q_hi) + nt(k_hi, q_lo) + nt(klo_ref[h, p], q_hi)

    iota = lax.broadcasted_iota(jnp.int32, (PEER_KEYS, TOK_TILE), 0)

    def sub_key_step(j, carry):
        for c in range(2 * TOPK_HEADS):
            s = s_scr[c]
            m, idx, hit = _extract_max(s, iota, PEER_KEYS)
            v_scr[c, pl.ds(j, 1), :] = m
            i_scr[c, pl.ds(j, 1), :] = idx
            s_scr[c] = jnp.where(hit, -jnp.inf, s)
        return carry

    lax.fori_loop(0, PEER_TOPK, sub_key_step, 0)

    row8 = lax.broadcasted_iota(jnp.int32, (8, TOK_TILE), 0)
    for h in range(TOPK_HEADS):
        v1, i1 = v_scr[2 * h], i_scr[2 * h] * PEER_KEYS
        v2, i2 = v_scr[2 * h + 1], i_scr[2 * h + 1]
        vals = [v1[0:1] + v2]
        idxs = [i1[0:1] + i2]
        for a in range(1, 8):
            vals.append(jnp.where(row8 < STAIR_COUNTS[a], v1[a:a + 1] + v2[0:8], -jnp.inf))
            idxs.append(i1[a:a + 1] + i2[0:8])
        vals.append(v1[8:16] + v2[0:1])
        idxs.append(i1[8:16] + i2[0:1])
        c_scr[h] = jnp.concatenate(vals, axis=0)
        ci_scr[h] = jnp.concatenate(idxs, axis=0)

    iota_c = lax.broadcasted_iota(jnp.int32, (STAIR_ROWS, TOK_TILE), 0)

    def expert_step(j, carry):
        for h in range(TOPK_HEADS):
            s = c_scr[h]
            m, _, hit = _extract_max(s, iota_c, STAIR_ROWS)
            sc_scr[h, pl.ds(j, 1), :] = m
            e_out[pl.ds(h * PEER_TOPK + j, 1), :] = jnp.max(jnp.where(hit, ci_scr[h], -1), axis=0,
                                                             keepdims=True)
            c_scr[h] = jnp.where(hit, -jnp.inf, s)
        return carry

    lax.fori_loop(0, PEER_TOPK, expert_step, 0)

    for h in range(TOPK_HEADS):
        sc = sc_scr[h]
        e = jnp.exp(sc - sc[0:1])
        g_out[h * PEER_TOPK:(h + 1) * PEER_TOPK, :] = e / jnp.sum(e, axis=0, keepdims=True)


def _topk_call(q, khi, klo):
    n = q.shape[0]
    rows = TOPK_HEADS * PEER_TOPK
    sel = pl.BlockSpec((rows, TOK_TILE), lambda i, h: (h, i))
    keys = pl.BlockSpec((TOPK_HEADS, 2, PEER_KEYS, PEER_HALF), lambda i, h: (h, 0, 0, 0))
    chains = 2 * TOPK_HEADS
    return pl.pallas_call(
        _topk_kernel, grid=(n // TOK_TILE, PEER_HEADS // TOPK_HEADS),
        in_specs=[pl.BlockSpec((TOK_TILE, chains * PEER_HALF), lambda i, h: (i, h)), keys, keys],
        out_specs=[sel, sel],
        out_shape=[jax.ShapeDtypeStruct((PEER_SEL, n), jnp.int32), jax.ShapeDtypeStruct((PEER_SEL, n), F32)],
        scratch_shapes=[pltpu.VMEM((chains, PEER_KEYS, TOK_TILE), F32),
                        pltpu.VMEM((chains, PEER_TOPK, TOK_TILE), F32),
                        pltpu.VMEM((chains, PEER_TOPK, TOK_TILE), jnp.int32),
                        pltpu.VMEM((TOPK_HEADS, STAIR_ROWS, TOK_TILE), F32),
                        pltpu.VMEM((TOPK_HEADS, STAIR_ROWS, TOK_TILE), jnp.int32),
                        pltpu.VMEM((TOPK_HEADS, PEER_TOPK, TOK_TILE), F32)],
        compiler_params=pltpu.CompilerParams(dimension_semantics=("arbitrary", "arbitrary")),
        name="topk",
    )(q, khi, klo)


def _pack_kernel(u_ref, v_ref, w_out):
    ub = lax.bitcast_convert_type(u_ref[...].astype(BF16).astype(F32), jnp.uint32)
    vb = lax.bitcast_convert_type(v_ref[...].astype(BF16).astype(F32), jnp.uint32)
    w_out[...] = (ub & jnp.uint32(0xFFFF0000)) | (vb >> 16)


def _pack_call(u, v):
    n, d = u.shape
    rows = 512
    blk = pl.BlockSpec((rows, d), lambda i: (i, 0))
    return pl.pallas_call(
        _pack_kernel, grid=(n // rows,), in_specs=[blk, blk], out_specs=blk,
        out_shape=jax.ShapeDtypeStruct((n, d), jnp.uint32),
        compiler_params=pltpu.CompilerParams(dimension_semantics=("arbitrary",)),
        name="pack",
    )(u, v)


def _mix_tokens(plane, xn2_rows, gate_rows, x2_rows, nfg, between):
    n = len(xn2_rows)
    ones = jnp.ones((128, 128), BF16)
    sums = [None] * n
    step = 0
    for s in range(8):
        for k in range(n):
            u = lax.bitcast_convert_type(plane(k, s) & jnp.uint32(0xFFFF0000), F32)
            term = u * xn2_rows[k][:, s * 128:(s + 1) * 128]
            sums[k] = term if s == 0 else sums[k] + term
            between(step)
            step += 1
    coefs = []
    for k in range(n):
        act = _dot_hl(sums[k], ones).T[0:8, :]
        gelu = 0.5 * act * (1.0 + jnp.tanh(0.7978845608028654 * (act + 0.044715 * (act * act * act))))
        coef = gate_rows[k] * gelu
        coefs.append(jnp.broadcast_to(coef[0:1, :], (128, PEER_SEL)).T)
    outs = [[] for _ in range(n)]
    for s in range(8):
        for k in range(n):
            v = lax.bitcast_convert_type(plane(k, s) << 16, F32)
            mix = jnp.sum(coefs[k] * v, axis=0, keepdims=True)
            outs[k].append(x2_rows[k][:, s * 128:(s + 1) * 128] + mix)
            between(step)
            step += 1
    return [_rms_norm(jnp.concatenate(outs[k], axis=1), nfg) for k in range(n)]


def _peer_kernel(e_ref, g_ref, n2g_ref, x2_ref, nfg_ref, w_hbm, y_out,
                 e_smem, g_vmem, wbuf, sem_e, sem_w):
    to_smem = pltpu.make_async_copy(e_ref, e_smem, sem_e)
    to_smem.start()
    g_vmem[...] = g_ref[...].T
    to_smem.wait()

    def row_copy(t, slot, j):
        return pltpu.make_async_copy(w_hbm.at[e_smem[j, t]], wbuf.at[slot, :, j, :], sem_w.at[slot])

    def wait_slot(slot):
        pltpu.make_async_copy(wbuf.at[(slot + 1) % PEER_SLOTS], wbuf.at[slot], sem_w.at[slot]).wait()

    def mix_pair(t, slots, t_next, next_slots):
        starts = [(k, j) for k in range(2) for j in range(PEER_SEL)]
        per_step = len(starts) // 32

        def start_some(step):
            for k, j in starts[step * per_step:(step + 1) * per_step]:
                row_copy(t_next + k, next_slots[k], j).start(priority=j % 2)

        rows = lambda ref: [ref[pl.ds(t + k, 1), :] for k in range(2)]
        x2_rows = rows(x2_ref)
        xn2_rows = [_rms_norm(r, n2g_ref[...]) for r in x2_rows]
        ys = _mix_tokens(lambda k, s: wbuf[slots[k], s], xn2_rows, rows(g_vmem), x2_rows,
                         nfg_ref[...], start_some)
        for k in range(2):
            y_out[pl.ds(t + k, 1), :] = ys[k]

    for j in range(PEER_SEL):
        row_copy(0, 0, j).start(priority=j % 2)
        row_copy(1, 1, j).start(priority=j % 2)

    def body(i, carry):
        t = PEER_SLOTS * i
        wait_slot(0)
        wait_slot(1)
        mix_pair(t, (0, 1), t + 2, (2, 3))
        wait_slot(2)
        wait_slot(3)
        mix_pair(t + 2, (2, 3), jnp.minimum(t + 4, TOK_TILE - 2), (0, 1))
        return carry

    lax.fori_loop(0, TOK_TILE // PEER_SLOTS, body, 0)
    wait_slot(0)
    wait_slot(1)


def _peer_call(eidx, gate, n2g, x2, nfg, w3, first_tile, n_tiles):
    sel = pl.BlockSpec((PEER_SEL, TOK_TILE), lambda i: (0, i + first_tile))
    tok = pl.BlockSpec((TOK_TILE, D_MODEL), lambda i: (i + first_tile, 0))
    gain = pl.BlockSpec((1, D_MODEL), lambda i: (0, 0))
    return pl.pallas_call(
        _peer_kernel, grid=(n_tiles,),
        in_specs=[sel, sel, gain, tok, gain, pl.BlockSpec(memory_space=pl.ANY)],
        out_specs=pl.BlockSpec((TOK_TILE, D_MODEL), lambda i: (i, 0)),
        out_shape=jax.ShapeDtypeStruct((n_tiles * TOK_TILE, D_MODEL), F32),
        scratch_shapes=[pltpu.SMEM((PEER_SEL, TOK_TILE), jnp.int32),
                        pltpu.VMEM((TOK_TILE, PEER_SEL), F32),
                        pltpu.VMEM((PEER_SLOTS, 8, PEER_SEL, 128), jnp.uint32),
                        pltpu.SemaphoreType.DMA,
                        pltpu.SemaphoreType.DMA((PEER_SLOTS,))],
        compiler_params=pltpu.CompilerParams(dimension_semantics=("arbitrary",)),
        name="peer",
    )(eidx, gate, n2g, x2, nfg, w3)


def _sc_gather_call(w3, idx):
    rows = idx.shape[0]
    per_worker = rows // (SC_CORES * SC_SUBCORES)
    mesh = plsc.VectorSubcoreMesh(core_axis_name="c", subcore_axis_name="s",
                                  num_cores=SC_CORES, num_subcores=SC_SUBCORES)

    @functools.partial(
        pl.kernel, mesh=mesh, out_type=jax.ShapeDtypeStruct((rows, 8, 128), jnp.uint32),
        scratch_types=[pltpu.VMEM((SC_WINDOW,), jnp.int32),
                       pltpu.VMEM((SC_WINDOW, 8, 128), jnp.uint32),
                       pltpu.SemaphoreType.DMA],
        name="sc_gather")
    def gather(table_hbm, idx_hbm, out_hbm, idx_v, rows_v, sem):
        base = (lax.axis_index("s") * SC_CORES + lax.axis_index("c")) * per_worker

        @pl.loop(0, per_worker // SC_WINDOW)
        def _(win):
            off = base + win * SC_WINDOW
            pltpu.sync_copy(idx_hbm.at[pl.ds(off, SC_WINDOW)], idx_v)
            pltpu.async_copy(table_hbm.at[idx_v], rows_v, sem).wait()
            pltpu.sync_copy(rows_v, out_hbm.at[pl.ds(off, SC_WINDOW)])

    return gather(w3, idx)


def _peer_staged_kernel(g_ref, n2g_ref, x2_ref, nfg_ref, w_ref, y_out, g_vmem):
    sub = pl.program_id(1)

    @pl.when(sub == 0)
    def _():
        g_vmem[...] = g_ref[...].T

    plane = lambda k, s: w_ref[pl.ds(k * (8 * PEER_SEL) + s, PEER_SEL, stride=8), :]
    rows = lambda ref: [ref[k:k + 1, :] for k in range(STAGE_TOKENS)]
    gates = [g_vmem[pl.ds(sub * STAGE_TOKENS + k, 1), :] for k in range(STAGE_TOKENS)]
    x2_rows = rows(x2_ref)
    xn2_rows = [_rms_norm(r, n2g_ref[...]) for r in x2_rows]
    ys = _mix_tokens(plane, xn2_rows, gates, x2_rows, nfg_ref[...], lambda step: None)
    y_out[...] = jnp.concatenate(ys, axis=0)


def _peer_staged_call(gate, n2g, x2, nfg, staged, first_tile, n_tiles):
    subs = TOK_TILE // STAGE_TOKENS
    tok = pl.BlockSpec((STAGE_TOKENS, D_MODEL), lambda i, j: ((i + first_tile) * subs + j, 0))
    staged2d = staged.reshape(-1, 128)
    return pl.pallas_call(
        _peer_staged_kernel, grid=(n_tiles, subs),
        in_specs=[pl.BlockSpec((PEER_SEL, TOK_TILE), lambda i, j: (0, i + first_tile)),
                  pl.BlockSpec((1, D_MODEL), lambda i, j: (0, 0)), tok,
                  pl.BlockSpec((1, D_MODEL), lambda i, j: (0, 0)),
                  pl.BlockSpec((STAGE_TOKENS * PEER_SEL * 8, 128),
                               lambda i, j: (i * subs + j, 0))],
        out_specs=pl.BlockSpec((STAGE_TOKENS, D_MODEL), lambda i, j: (i * subs + j, 0)),
        out_shape=jax.ShapeDtypeStruct((n_tiles * TOK_TILE, D_MODEL), F32),
        scratch_shapes=[pltpu.VMEM((TOK_TILE, PEER_SEL), F32)],
        compiler_params=pltpu.CompilerParams(dimension_semantics=("arbitrary", "arbitrary"),
                                             vmem_limit_bytes=VMEM_LIMIT_BYTES),
        name="peer_staged",
    )(gate, n2g, x2, nfg, staged2d)


def _tile_choices(bsz, seq):
    tt = min(seq, 256)
    chunk = min(seq, 64)
    post = min(bsz * seq, 256)
    return tt, chunk, post


def _mixer_stage(x, b0, bsz, st_conv, st_shift, st_wkv, w):
    seq = x.shape[1]
    n = bsz * seq
    tt, chunk, post_tt = _tile_choices(bsz, seq)
    (r, lw, k, v, a, b, bonus, g, ma, sgb, new_shift, new_conv) = _pre_call(
        x, b0, bsz, st_conv, st_shift.reshape(bsz, 1, D_MODEL), w, tt)
    y, new_wkv = _wkv_call(r, lw, k, v, a, b, st_wkv, chunk)
    flat = lambda t: t.reshape(-1, t.shape[-1])
    x2, q = _post_call(flat(y), flat(bonus), flat(g), flat(ma), flat(sgb), flat(x), b0 * seq, w,
                            post_tt)
    eidx, gate = _topk_call(q, w["keys_hi"], w["keys_lo"])
    states = (new_conv[None], new_shift.reshape(1, bsz, D_MODEL), new_wkv[None])
    return ((bsz, seq, D_MODEL), eidx, gate, x2), states


def _peer_stage(sel, w):
    (bsz, seq, _), eidx, gate, x2 = sel
    tiles = bsz * seq // TOK_TILE
    if tiles < SC_MIN_TILES:
        out = _peer_call(eidx, gate, w["norm2_g"], x2, w["norm_f_g"], w["peer_w"], 0, tiles)
        return out.reshape(bsz, seq, D_MODEL)
    bounds = list(range(0, tiles, SC_CHUNK_TILES)) + [tiles]
    staged = [_sc_gather_call(w["peer_w"], eidx[:, lo * TOK_TILE:hi * TOK_TILE].T.reshape(-1))
              for lo, hi in zip(bounds[:-1], bounds[1:])]
    pieces = [_peer_staged_call(gate, w["norm2_g"], x2, w["norm_f_g"], rows, lo, hi - lo)
              for rows, lo, hi in zip(staged, bounds[:-1], bounds[1:])]
    return jnp.concatenate(pieces).reshape(bsz, seq, D_MODEL)


def kernel(x_prompt, x_sample, state_conv, state_shift, state_wkv, norm1_g, w_in, conv_w, mu_rkv, mu_wag,
           w0, w1, w2, a0, a1, a2, g1, g2, k_k, k_a, r_k, gn_w, gn_b, w_pa, w_pb, w_o, norm2_g,
           peer_wq, peer_keys, peer_u, peer_v, norm_f_g):
    row = lambda t: t.reshape(1, -1)
    head = jnp.arange(D_RWKV) // HEAD_DIM
    wq_hi, wq_lo = _split(peer_wq[0])
    keys = peer_keys[0]
    keys_hi, keys_lo = _split(keys)
    w = dict(
        norm1_g=norm1_g, w_in=w_in[0].astype(BF16), conv_w=conv_w[0], mu_rkv=mu_rkv, mu_wag=mu_wag[0],
        w0=w0, w1=w1[0].astype(BF16), w2=w2[0].astype(BF16), a0=a0, a1=a1[0].astype(BF16),
        a2=a2[0].astype(BF16), g1=g1[0].astype(BF16), g2=g2[0].astype(BF16), k_k=k_k, k_a=k_a,
        r_k=row(r_k[0]), gn_w=gn_w, gn_b=gn_b, w_pa=w_pa[0].astype(BF16), w_pb=w_pb[0].astype(BF16),
        w_o=w_o[0].astype(BF16), norm2_g=norm2_g, wq_hi=wq_hi, wq_lo=wq_lo, keys_hi=keys_hi,
        keys_lo=keys_lo, norm_f_g=row(norm_f_g),
        peer_w=_pack_call(peer_u[0], peer_v[0]).reshape(-1, 8, 128),
        hsum=(head[:, None] == head[None, :]).astype(BF16),
    )
    bp = x_prompt.shape[0]
    per = bp // PROMPT_GROUPS
    zero_conv = jnp.zeros((per, CONV_W - 1, D_CONV), F32)
    zero_shift = jnp.zeros((per, D_MODEL), F32)
    zero_wkv = jnp.zeros((per, HEADS, HEAD_DIM, HEAD_DIM), F32)
    groups = [(x_sample, 0, x_sample.shape[0], state_conv[0], state_shift[0], state_wkv[0])]
    groups += [(x_prompt, i * per, per, zero_conv, zero_shift, zero_wkv) for i in range(PROMPT_GROUPS)]
    mixed = [_mixer_stage(*grp, w) for grp in groups]
    ys = [_peer_stage(sel, w) for sel, _ in mixed]
    (conv_s, shift_s, wkv_s), prompt_states = mixed[0][1], [st for _, st in mixed[1:]]
    conv_p, shift_p, wkv_p = (jnp.concatenate(parts, axis=1) for parts in zip(*prompt_states))
    return (jnp.concatenate(ys[1:], axis=0), ys[0], conv_p, shift_p, wkv_p, conv_s, shift_s, wkv_s)
```
